```python
import math
import jax, jax.numpy as jnp
from jax import lax
import numpy as np

D_MODEL = 2048
BATCH = 4
SEQ = 2048
DEPTH = 4
DEC_BATCH = 8
DEC_SEQ = 4
PAST_LEN = 16384
PAGE_SIZE = 128

N_S5 = (DEPTH + 1) // 2
N_ATTN = DEPTH // 2
GROUP_WIDTH = 16
N_GROUPS = D_MODEL // GROUP_WIDTH
STATE_DIM = 64
N_HEADS = 16
HEAD_DIM = 64
QK_DIM = 2 * HEAD_DIM
V_DIM = 2 * HEAD_DIM
D_FF = 5632
PLE_DIM = 256
Q_BLOCK = 128
ALPHA = (2.0 * DEPTH) ** 0.25
BETA = (8.0 * DEPTH) ** -0.25
LN_EPS = 1e-5
NEG_INF = -1e30
ATTN_SCALE = 1.0 / math.sqrt(HEAD_DIM)

kernel_name = "s5_diffattn_interleaved_decoder_step"


def _layer_norm(x, g, b):
    xf = x.astype(jnp.float32)
    mu = jnp.mean(xf, axis=-1, keepdims=True)
    var = jnp.mean(jnp.square(xf - mu), axis=-1, keepdims=True)
    return ((xf - mu) * lax.rsqrt(var + LN_EPS) * g + b).astype(x.dtype)


def _swiglu(x, wg, wu, wd):
    return (jax.nn.silu(x @ wg) * (x @ wu)) @ wd


def _layer(x, p_i, i, mix, ln_g, ln_b, w_ffn_gate, w_ffn_up, w_ffn_down, w_pe, w_pe_gate):
    h = _layer_norm(ALPHA * x + 0.5 * _swiglu(x, w_ffn_gate[i, 0], w_ffn_up[i, 0], w_ffn_down[i, 0]), ln_g[i, 0], ln_b[i, 0])
    y, st = mix(h)
    h = _layer_norm(ALPHA * h + y, ln_g[i, 1], ln_b[i, 1])
    h = _layer_norm(ALPHA * h + 0.5 * _swiglu(h, w_ffn_gate[i, 1], w_ffn_up[i, 1], w_ffn_down[i, 1]), ln_g[i, 2], ln_b[i, 2])
    h = h + jax.nn.sigmoid(h @ w_pe_gate[i]) * (p_i @ w_pe[i])
    return h, st


def _complex_linear_combine(e1, e2):
    ar1, ai1, br1, bi1 = e1
    ar2, ai2, br2, bi2 = e2
    return (ar1 * ar2 - ai1 * ai2,
            ar1 * ai2 + ai1 * ar2,
            ar2 * br1 - ai2 * bi1 + br2,
            ar2 * bi1 + ai2 * br1 + bi2)


def _s5_mixer(u, s0_re, s0_im, lam_re, lam_im, log_dt, b_re, b_im, c_re, c_im, d_skip, glu_w, glu_b, w_out):
    f32 = jnp.float32
    bsz, seq, _ = u.shape
    ug = u.astype(f32).reshape(bsz, seq, N_GROUPS, GROUP_WIDTH)
    lr, li = lam_re.astype(f32), lam_im.astype(f32)
    dt = jnp.exp(log_dt.astype(f32))[:, None]

    mag = jnp.exp(lr * dt)
    ang = li * dt
    a_re, a_im = mag * jnp.cos(ang), mag * jnp.sin(ang)
    den = lr * lr + li * li
    coef_re = ((a_re - 1.0) * lr + a_im * li) / den
    coef_im = (a_im * lr - (a_re - 1.0) * li) / den
    br, bi = b_re.astype(f32), b_im.astype(f32)
    bb_re = coef_re[..., None] * br - coef_im[..., None] * bi
    bb_im = coef_re[..., None] * bi + coef_im[..., None] * br

    bu_re = jnp.einsum('blgw,gpw->lbgp', ug, bb_re)
    bu_im = jnp.einsum('blgw,gpw->lbgp', ug, bb_im)
    if s0_re is not None:
        s0r, s0i = s0_re.astype(f32), s0_im.astype(f32)
        bu_re = bu_re.at[0].add(a_re * s0r - a_im * s0i)
        bu_im = bu_im.at[0].add(a_re * s0i + a_im * s0r)
    a_re_t = jnp.broadcast_to(a_re, (seq, 1) + a_re.shape)
    a_im_t = jnp.broadcast_to(a_im, (seq, 1) + a_im.shape)
    _, _, s_re, s_im = lax.associative_scan(_complex_linear_combine, (a_re_t, a_im_t, bu_re, bu_im), axis=0)

    y = (jnp.einsum('lbgp,gwp->blgw', s_re, c_re.astype(f32))
         - jnp.einsum('lbgp,gwp->blgw', s_im, c_im.astype(f32)))
    y = y.reshape(bsz, seq, D_MODEL) + d_skip.astype(f32) * u.astype(f32)
    y = jax.nn.gelu(y).astype(u.dtype)
    y = y * jax.nn.sigmoid(y @ glu_w + glu_b)
    return y @ w_out, (s_re[-1].astype(u.dtype), s_im[-1].astype(u.dtype))


def _alibi_slopes():
    return jnp.exp2(-8.0 * jnp.arange(1, N_HEADS + 1, dtype=jnp.float32) / N_HEADS)


def _diff_qkv(h, w_qkv):
    b, l, _ = h.shape
    qkv = h @ w_qkv
    q, k, v = jnp.split(qkv, [N_HEADS * QK_DIM, 2 * N_HEADS * QK_DIM], axis=-1)
    return (q.reshape(b, l, N_HEADS, 2, HEAD_DIM),
            k.reshape(b, l, N_HEADS, QK_DIM),
            v.reshape(b, l, N_HEADS, V_DIM))


def _diff_lambda(lq1, lk1, lq2, lk2, lam_init):
    f32 = jnp.float32
    return (jnp.exp(jnp.sum(lq1.astype(f32) * lk1.astype(f32)))
            - jnp.exp(jnp.sum(lq2.astype(f32) * lk2.astype(f32))) + lam_init)


def _diff_out(o, lam_init, subln_g, w_o, dtype):
    b, l = o.shape[:2]
    o = o * lax.rsqrt(jnp.mean(o * o, axis=-1, keepdims=True) + LN_EPS) * subln_g.astype(jnp.float32)
    o = o * (1.0 - lam_init)
    return o.reshape(b, l, N_HEADS * V_DIM).astype(dtype) @ w_o


def _diff_attn_prompt(h, lam_init, w_qkv, lq1, lk1, lq2, lk2, subln_g, w_o):
    f32 = jnp.float32
    b, l, _ = h.shape
    q, k, v = _diff_qkv(h, w_qkv)
    lam = _diff_lambda(lq1, lk1, lq2, lk2, lam_init)
    slopes = _alibi_slopes()
    k4 = k.astype(f32).reshape(b, l, N_HEADS, 2, HEAD_DIM)
    k1, k2 = k4[..., 0, :], k4[..., 1, :]
    vf = v.astype(f32)
    nqb = l // Q_BLOCK
    qb = jnp.moveaxis(q.astype(f32).reshape(b, nqb, Q_BLOCK, N_HEADS, 2, HEAD_DIM), 1, 0)
    key_pos = jnp.arange(l)

    def block(args):
        qblk, qi = args
        qpos = qi * Q_BLOCK + jnp.arange(Q_BLOCK)
        dist = (qpos[:, None] - key_pos[None, :]).astype(f32)
        bias = -slopes[:, None, None] * dist
        causal = dist >= 0
        s1 = jnp.einsum('bthd,bshd->bhts', qblk[..., 0, :], k1) * ATTN_SCALE + bias
        s2 = jnp.einsum('bthd,bshd->bhts', qblk[..., 1, :], k2) * ATTN_SCALE + bias
        p1 = jax.nn.softmax(jnp.where(causal, s1, NEG_INF), axis=-1)
        p2 = jax.nn.softmax(jnp.where(causal, s2, NEG_INF), axis=-1)
        return jnp.einsum('bhts,bshd->bthd', p1 - lam * p2, vf)

    o = lax.map(block, (qb, jnp.arange(nqb)))
    o = jnp.moveaxis(o, 0, 1).reshape(b, l, N_HEADS, V_DIM)
    return _diff_out(o, lam_init, subln_g, w_o, h.dtype), (k, v)


def _online_update(carry, scores, v):
    out = []
    for (m, l, acc), s in zip(carry, scores):
        m_new = jnp.maximum(m, jnp.max(s, axis=-1))
        corr = jnp.exp(m - m_new)
        p = jnp.exp(s - m_new[..., None])
        out.append((m_new, l * corr + jnp.sum(p, axis=-1),
                    acc * corr[..., None] + jnp.einsum('bhts,bshd->bhtd', p, v)))
    return (out[0], out[1])


def _diff_attn_sample(h, cache_k, cache_v, page_table, j, lam_init, w_qkv, lq1, lk1, lq2, lk2, subln_g, w_o):
    f32 = jnp.float32
    b, t, _ = h.shape
    n_pages = page_table.shape[1]
    page = cache_k.shape[2]
    past = n_pages * page
    q, k, v = _diff_qkv(h, w_qkv)
    lam = _diff_lambda(lq1, lk1, lq2, lk2, lam_init)
    slopes = _alibi_slopes()
    qf = q.astype(f32)
    q1, q2 = qf[..., 0, :], qf[..., 1, :]
    qpos = past + jnp.arange(t)

    def scores(kk, kpos):
        kk = kk.astype(f32).reshape(b, -1, N_HEADS, 2, HEAD_DIM)
        bias = -slopes[:, None, None] * (qpos[:, None] - kpos[None, :]).astype(f32)
        s1 = jnp.einsum('bthd,bshd->bhts', q1, kk[..., 0, :]) * ATTN_SCALE + bias
        s2 = jnp.einsum('bthd,bshd->bhts', q2, kk[..., 1, :]) * ATTN_SCALE + bias
        return s1, s2

    def fresh():
        return (jnp.full((b, N_HEADS, t), NEG_INF, f32), jnp.zeros((b, N_HEADS, t), f32),
                jnp.zeros((b, N_HEADS, t, V_DIM), f32))
    init = (fresh(), fresh())

    def step(carry, xs):
        phys, pidx = xs
        s1, s2 = scores(cache_k[j, phys], pidx * page + jnp.arange(page))
        return _online_update(carry, (s1, s2), cache_v[j, phys].astype(f32)), None

    carry, _ = lax.scan(step, init, (page_table.T, jnp.arange(n_pages)))
    s1, s2 = scores(k, qpos)
    causal = qpos[:, None] >= qpos[None, :]
    carry = _online_update(carry, (jnp.where(causal, s1, NEG_INF), jnp.where(causal, s2, NEG_INF)), v.astype(f32))
    (_, l1, a1), (_, l2, a2) = carry
    o = a1 / l1[..., None] - lam * (a2 / l2[..., None])
    o = jnp.transpose(o, (0, 2, 1, 3))
    return _diff_out(o, lam_init, subln_g, w_o, h.dtype), (k, v)


def setup_inputs(seed: int = 0) -> dict:
    key = jax.random.key(seed)
    ks = iter(jax.random.split(key, 48))
    f32 = jnp.float32

    def nrm(shape, scale):
        return scale * jax.random.normal(next(ks), shape, f32)

    n_pages = PAST_LEN // PAGE_SIZE
    n_used = DEC_BATCH * n_pages
    n_pool = n_used + n_used // 4
    page_table = jax.random.permutation(next(ks), n_pool)[:n_used].reshape(DEC_BATCH, n_pages).astype(jnp.int32)
    lam_im_init = jnp.pi * jnp.arange(STATE_DIM, dtype=f32)
    return {
        "x_prompt": nrm((BATCH, SEQ, D_MODEL), 1.0),
        "x_sample": nrm((DEC_BATCH, DEC_SEQ, D_MODEL), 1.0),
        "cache_k": nrm((N_ATTN, n_pool, PAGE_SIZE, N_HEADS, QK_DIM), 1.0),
        "cache_v": nrm((N_ATTN, n_pool, PAGE_SIZE, N_HEADS, V_DIM), 1.0),
        "state_s5_re": nrm((N_S5, DEC_BATCH, N_GROUPS, STATE_DIM), 0.1),
        "state_s5_im": nrm((N_S5, DEC_BATCH, N_GROUPS, STATE_DIM), 0.1),
        "page_table": page_table,
        "p_prompt": nrm((DEPTH, BATCH, SEQ, PLE_DIM), 1.0),
        "p_sample": nrm((DEPTH, DEC_BATCH, DEC_SEQ, PLE_DIM), 1.0),
        "ln_g": 1.0 + nrm((DEPTH, 3, D_MODEL), 0.02),
        "ln_b": nrm((DEPTH, 3, D_MODEL), 0.02),
        "w_ffn_gate": nrm((DEPTH, 2, D_MODEL, D_FF), D_MODEL ** -0.5),
        "w_ffn_up": nrm((DEPTH, 2, D_MODEL, D_FF), D_MODEL ** -0.5),
        "w_ffn_down": nrm((DEPTH, 2, D_FF, D_MODEL), BETA * D_FF ** -0.5),
        "w_pe": nrm((DEPTH, PLE_DIM, D_MODEL), PLE_DIM ** -0.5),
        "w_pe_gate": nrm((DEPTH, D_MODEL, D_MODEL), D_MODEL ** -0.5),
        "s5_lambda_re": -0.5 + nrm((N_S5, N_GROUPS, STATE_DIM), 0.01),
        "s5_lambda_im": lam_im_init + nrm((N_S5, N_GROUPS, STATE_DIM), 0.01),
        "s5_log_dt": jax.random.uniform(next(ks), (N_S5, N_GROUPS), f32, math.log(1e-3), math.log(1e-1)),
        "s5_b_re": nrm((N_S5, N_GROUPS, STATE_DIM, GROUP_WIDTH), (2.0 * GROUP_WIDTH) ** -0.5),
        "s5_b_im": nrm((N_S5, N_GROUPS, STATE_DIM, GROUP_WIDTH), (2.0 * GROUP_WIDTH) ** -0.5),
        "s5_c_re": nrm((N_S5, N_GROUPS, GROUP_WIDTH, STATE_DIM), STATE_DIM ** -0.5),
        "s5_c_im": nrm((N_S5, N_GROUPS, GROUP_WIDTH, STATE_DIM), STATE_DIM ** -0.5),
        "s5_d": nrm((N_S5, D_MODEL), 1.0),
        "s5_glu_w": nrm((N_S5, D_MODEL, D_MODEL), D_MODEL ** -0.5),
        "s5_glu_b": nrm((N_S5, D_MODEL), 0.02),
        "s5_w_out": nrm((N_S5, D_MODEL, D_MODEL), BETA * D_MODEL ** -0.5),
        "attn_w_qkv": nrm((N_ATTN, D_MODEL, N_HEADS * (2 * QK_DIM + V_DIM)), D_MODEL ** -0.5),
        "attn_lambda_q1": nrm((N_ATTN, HEAD_DIM), 0.1),
        "attn_lambda_k1": nrm((N_ATTN, HEAD_DIM), 0.1),
        "attn_lambda_q2": nrm((N_ATTN, HEAD_DIM), 0.1),
        "attn_lambda_k2": nrm((N_ATTN, HEAD_DIM), 0.1),
        "attn_subln_g": 1.0 + nrm((N_ATTN, V_DIM), 0.02),
        "attn_w_o": nrm((N_ATTN, N_HEADS * V_DIM, D_MODEL), BETA * (N_HEADS * V_DIM) ** -0.5),
    }


def reference(x_prompt, x_sample, cache_k, cache_v, state_s5_re, state_s5_im, page_table, p_prompt, p_sample,
              ln_g, ln_b, w_ffn_gate, w_ffn_up, w_ffn_down, w_pe, w_pe_gate,
              s5_lambda_re, s5_lambda_im, s5_log_dt, s5_b_re, s5_b_im, s5_c_re, s5_c_im, s5_d,
              s5_glu_w, s5_glu_b, s5_w_out,
              attn_w_qkv, attn_lambda_q1, attn_lambda_k1, attn_lambda_q2, attn_lambda_k2, attn_subln_g, attn_w_o):
    ffn = (ln_g, ln_b, w_ffn_gate, w_ffn_up, w_ffn_down, w_pe, w_pe_gate)
    hp, hs = x_prompt, x_sample
    kp, vp, ksm, vsm = [], [], [], []
    srp, sip, srs, sis = [], [], [], []
    for i in range(DEPTH):
        j = i // 2
        if i % 2 == 0:
            prm = (s5_lambda_re[j], s5_lambda_im[j], s5_log_dt[j], s5_b_re[j], s5_b_im[j], s5_c_re[j],
                   s5_c_im[j], s5_d[j], s5_glu_w[j], s5_glu_b[j], s5_w_out[j])
            hp, (a, b) = _layer(hp, p_prompt[i], i, lambda h: _s5_mixer(h, None, None, *prm), *ffn)
            srp.append(a)
            sip.append(b)
            hs, (a, b) = _layer(hs, p_sample[i], i,
                                lambda h: _s5_mixer(h, state_s5_re[j], state_s5_im[j], *prm), *ffn)
            srs.append(a)
            sis.append(b)
        else:
            lam_init = 0.8 - 0.6 * math.exp(-0.3 * i)
            prm = (attn_w_qkv[j], attn_lambda_q1[j], attn_lambda_k1[j], attn_lambda_q2[j], attn_lambda_k2[j],
                   attn_subln_g[j], attn_w_o[j])
            hp, (a, b) = _layer(hp, p_prompt[i], i, lambda h: _diff_attn_prompt(h, lam_init, *prm), *ffn)
            kp.append(a)
            vp.append(b)
            hs, (a, b) = _layer(hs, p_sample[i], i,
                                lambda h: _diff_attn_sample(h, cache_k, cache_v, page_table, j, lam_init, *prm), *ffn)
            ksm.append(a)
            vsm.append(b)
    return (hp, hs, jnp.stack(kp), jnp.stack(vp), jnp.stack(ksm), jnp.stack(vsm),
            jnp.stack(srp), jnp.stack(sip), jnp.stack(srs), jnp.stack(sis))
```

```python
import functools
import math

import jax
import jax.numpy as jnp
from jax import lax
from jax.experimental import pallas as pl
from jax.experimental.pallas import tpu as pltpu

F32 = jnp.float32
BF16 = jnp.bfloat16
LN_EPS = 1e-5
NEG_INF = -1e30
LANES = 128
SUBLANES = 8
VMEM_LIMIT_BIG = 56 * 1024 * 1024
VMEM_LIMIT_SMALL = 40 * 1024 * 1024
EPILOGUE_ROWS = 128


def _pick(n, cands):
    for c in cands:
        if n % c == 0:
            return c
    return n


def _row_tile(m):
    return _pick(m, (1024, 512, 256, 128, 64, 32, 16, 8))


def _params(sem, vmem=VMEM_LIMIT_BIG):
    return pltpu.CompilerParams(dimension_semantics=sem, vmem_limit_bytes=vmem)


def _dot(a, b):
    return jnp.dot(a, b, preferred_element_type=F32)


def _dot_nt(a, b):
    return lax.dot_general(a, b, (((1,), (1,)), ((), ())), preferred_element_type=F32)


def _layer_norm(z, g, b):
    mu = jnp.mean(z, axis=-1, keepdims=True)
    zc = z - mu
    var = jnp.mean(zc * zc, axis=-1, keepdims=True)
    return zc * lax.rsqrt(var + LN_EPS) * g + b


def _ln_epilogue(o_ref, res_ref, g_ref, b_ref, alpha, scale):
    tm = o_ref.shape[0]
    rb = min(EPILOGUE_ROWS, tm)
    g = g_ref[...]
    b = b_ref[...]

    def chunk(r, carry):
        rows = pl.ds(pl.multiple_of(r * rb, rb), rb)
        z = alpha * res_ref[rows, :].astype(F32) + scale * o_ref[rows, :]
        o_ref[rows, :] = _layer_norm(z, g, b)
        return carry

    lax.fori_loop(0, tm // rb, chunk, 0)


def _ffn_ln_body(x_ref, wg_ref, wu_ref, wd_ref, g_ref, b_ref, o_ref, xb_ref, *, alpha, nf, col_chunk):
    j = pl.program_id(1)

    @pl.when(j == 0)
    def _():
        xb_ref[...] = x_ref[...].astype(BF16)
        o_ref[...] = jnp.zeros_like(o_ref)

    xb = xb_ref[...]
    gate = _dot(xb, wg_ref[...].astype(BF16))
    up = _dot(xb, wu_ref[...].astype(BF16))
    h = (gate * jax.nn.sigmoid(gate) * up).astype(BF16)
    d = o_ref.shape[1]
    for c in range(d // col_chunk):
        cols = slice(c * col_chunk, (c + 1) * col_chunk)
        o_ref[:, cols] += _dot(h, wd_ref[:, cols].astype(BF16))

    @pl.when(j == nf - 1)
    def _():
        _ln_epilogue(o_ref, x_ref, g_ref, b_ref, alpha, 0.5)


def _ffn_ln(x, w_gate, w_up, w_down, ln_g4, ln_b4, li, si, ln_idx, alpha):
    m, d = x.shape
    f = w_gate.shape[-1]
    tm = _row_tile(m)
    tf = _pick(f, (256, 128)) if tm >= 512 else _pick(f, (512, 256, 128))
    nf = f // tf
    body = functools.partial(_ffn_ln_body, alpha=alpha, nf=nf, col_chunk=_pick(d, (512, 256, 128)))
    return pl.pallas_call(
        body,
        grid=(m // tm, nf),
        in_specs=[
            pl.BlockSpec((tm, d), lambda i, j: (i, 0)),
            pl.BlockSpec((None, None, d, tf), lambda i, j: (li, si, 0, j)),
            pl.BlockSpec((None, None, d, tf), lambda i, j: (li, si, 0, j)),
            pl.BlockSpec((None, None, tf, d), lambda i, j: (li, si, j, 0)),
            pl.BlockSpec((None, None, 1, d), lambda i, j: (li, ln_idx, 0, 0)),
            pl.BlockSpec((None, None, 1, d), lambda i, j: (li, ln_idx, 0, 0)),
        ],
        out_specs=pl.BlockSpec((tm, d), lambda i, j: (i, 0)),
        out_shape=jax.ShapeDtypeStruct((m, d), F32),
        scratch_shapes=[pltpu.VMEM((tm, d), BF16)],
        compiler_params=_params(("parallel", "arbitrary")),
        name="ffn_ln",
    )(x, w_gate, w_up, w_down, ln_g4, ln_b4)


def _glu_out_ln_body(y_ref, res_ref, wglu_ref, bglu_ref, wout_ref, g_ref, b_ref, o_ref, *, alpha, nf, tf,
                     col_chunk):
    j = pl.program_id(1)

    @pl.when(j == 0)
    def _():
        o_ref[...] = jnp.zeros_like(o_ref)

    t = _dot(y_ref[...], wglu_ref[...].astype(BF16)) + bglu_ref[...]
    ycols = y_ref[:, pl.ds(pl.multiple_of(j * tf, tf), tf)].astype(F32)
    z = (ycols * jax.nn.sigmoid(t)).astype(BF16)
    d = o_ref.shape[1]
    for c in range(d // col_chunk):
        cols = slice(c * col_chunk, (c + 1) * col_chunk)
        o_ref[:, cols] += _dot(z, wout_ref[:, cols].astype(BF16))

    @pl.when(j == nf - 1)
    def _():
        _ln_epilogue(o_ref, res_ref, g_ref, b_ref, alpha, 1.0)


def _glu_out_ln(y, res, glu_w, glu_b3, w_out, ln_g4, ln_b4, sj, li, ln_idx, alpha):
    m, d = res.shape
    tm = _row_tile(m)
    tf = _pick(d, (256, 128))
    nf = d // tf
    body = functools.partial(_glu_out_ln_body, alpha=alpha, nf=nf, tf=tf, col_chunk=_pick(d, (512, 256, 128)))
    return pl.pallas_call(
        body,
        grid=(m // tm, nf),
        in_specs=[
            pl.BlockSpec((tm, d), lambda i, j: (i, 0)),
            pl.BlockSpec((tm, d), lambda i, j: (i, 0)),
            pl.BlockSpec((None, d, tf), lambda i, j: (sj, 0, j)),
            pl.BlockSpec((None, 1, tf), lambda i, j: (sj, 0, j)),
            pl.BlockSpec((None, tf, d), lambda i, j: (sj, j, 0)),
            pl.BlockSpec((None, None, 1, d), lambda i, j: (li, ln_idx, 0, 0)),
            pl.BlockSpec((None, None, 1, d), lambda i, j: (li, ln_idx, 0, 0)),
        ],
        out_specs=pl.BlockSpec((tm, d), lambda i, j: (i, 0)),
        out_shape=jax.ShapeDtypeStruct((m, d), F32),
        compiler_params=_params(("parallel", "arbitrary")),
        name="s5_glu_out_ln",
    )(y, res, glu_w, glu_b3, w_out, ln_g4, ln_b4)


def _proj_body(x_ref, w_ref, *refs, scale, n_out):
    outs, xb_ref = refs[:n_out], refs[n_out]
    j = pl.program_id(1)

    @pl.when(j == 0)
    def _():
        xb_ref[...] = x_ref[...].astype(BF16)

    r = _dot(xb_ref[...], w_ref[...].astype(BF16))
    if scale != 1.0:
        r = r * scale
    for o_ref in outs:
        o_ref[...] = r.astype(o_ref.dtype)


def _proj(x, w3, wj, col0, n, out_dtypes, scale=1.0):
    m, k = x.shape
    tm = _row_tile(m)
    tn = _pick(n, (512, 256, 128))
    cb = col0 // tn
    body = functools.partial(_proj_body, scale=scale, n_out=len(out_dtypes))
    return pl.pallas_call(
        body,
        grid=(m // tm, n // tn),
        in_specs=[
            pl.BlockSpec((tm, k), lambda i, j: (i, 0)),
            pl.BlockSpec((None, k, tn), lambda i, j: (wj, 0, cb + j)),
        ],
        out_specs=[pl.BlockSpec((tm, tn), lambda i, j: (i, j)) for _ in out_dtypes],
        out_shape=[jax.ShapeDtypeStruct((m, n), dt) for dt in out_dtypes],
        scratch_shapes=[pltpu.VMEM((tm, k), BF16)],
        compiler_params=_params(("parallel", "arbitrary")),
        name="proj",
    )(x, w3)


def _proj_ln_body(x_ref, res_ref, w_ref, g_ref, b_ref, o_ref, *, alpha, nn, tn):
    j = pl.program_id(1)
    o_ref[:, pl.ds(pl.multiple_of(j * tn, tn), tn)] = _dot(x_ref[...].astype(BF16), w_ref[...].astype(BF16))

    @pl.when(j == nn - 1)
    def _():
        _ln_epilogue(o_ref, res_ref, g_ref, b_ref, alpha, 1.0)


def _proj_ln(x, res, w3, wj, ln_g4, ln_b4, li, ln_idx, alpha):
    m, k = x.shape
    d = res.shape[1]
    tm = _row_tile(m)
    tn = _pick(d, (512, 256, 128))
    nn = d // tn
    body = functools.partial(_proj_ln_body, alpha=alpha, nn=nn, tn=tn)
    return pl.pallas_call(
        body,
        grid=(m // tm, nn),
        in_specs=[
            pl.BlockSpec((tm, k), lambda i, j: (i, 0)),
            pl.BlockSpec((tm, d), lambda i, j: (i, 0)),
            pl.BlockSpec((None, k, tn), lambda i, j: (wj, 0, j)),
            pl.BlockSpec((None, None, 1, d), lambda i, j: (li, ln_idx, 0, 0)),
            pl.BlockSpec((None, None, 1, d), lambda i, j: (li, ln_idx, 0, 0)),
        ],
        out_specs=pl.BlockSpec((tm, d), lambda i, j: (i, 0)),
        out_shape=jax.ShapeDtypeStruct((m, d), F32),
        compiler_params=_params(("parallel", "arbitrary")),
        name="proj_ln",
    )(x, res, w3, ln_g4, ln_b4)


def _pe_body(h_ref, p_ref, wg_ref, wp_ref, o_ref, hb_ref, *, tn):
    j = pl.program_id(1)

    @pl.when(j == 0)
    def _():
        hb_ref[...] = h_ref[...].astype(BF16)

    gate = _dot(hb_ref[...], wg_ref[...].astype(BF16))
    pe = _dot(p_ref[...].astype(BF16), wp_ref[...].astype(BF16))
    hcols = h_ref[:, pl.ds(pl.multiple_of(j * tn, tn), tn)]
    o_ref[...] = hcols + jax.nn.sigmoid(gate) * pe


def _pe_gate(h, p3, w_pe, w_pe_gate, li):
    m, d = h.shape
    pd = p3.shape[-1]
    tm = _row_tile(m)
    tn = _pick(d, (512, 256, 128))
    body = functools.partial(_pe_body, tn=tn)
    return pl.pallas_call(
        body,
        grid=(m // tm, d // tn),
        in_specs=[
            pl.BlockSpec((tm, d), lambda i, j: (i, 0)),
            pl.BlockSpec((None, tm, pd), lambda i, j: (li, i, 0)),
            pl.BlockSpec((None, d, tn), lambda i, j: (li, 0, j)),
            pl.BlockSpec((None, pd, tn), lambda i, j: (li, 0, j)),
        ],
        out_specs=pl.BlockSpec((tm, tn), lambda i, j: (i, j)),
        out_shape=jax.ShapeDtypeStruct((m, d), F32),
        scratch_shapes=[pltpu.VMEM((tm, d), BF16)],
        compiler_params=_params(("parallel", "arbitrary")),
        name="pe_gate",
    )(h, p3, w_pe_gate, w_pe)


def _cmul_add(sr, si, ar, ai, pr, pi):
    return sr + ar * pr - ai * pi, si + ar * pi + ai * pr


def _s5_body(*refs, rows, nlog, carry, has_s0, row_block):
    if has_s0:
        x_ref, bblk_ref, cblk_ref, coef_ref, d_ref, s0_ref, y_ref, st_ref, bu_ref = refs
    else:
        x_ref, bblk_ref, cblk_ref, coef_ref, d_ref, y_ref, st_ref, bu_ref = refs
    ns = bu_ref.shape[1] // 2
    re, im = slice(0, ns), slice(ns, 2 * ns)
    nblk = rows // row_block

    for rb in range(nblk):
        r = slice(rb * row_block, (rb + 1) * row_block)
        bu_ref[r, :] = _dot(x_ref[r, :].astype(BF16), bblk_ref[...])

    if has_s0:
        ar, ai = coef_ref[nlog, 0:1, re], coef_ref[nlog, 0:1, im]
        s0r, s0i = s0_ref[:, re], s0_ref[:, im]
        br, bi = _cmul_add(bu_ref[:, re], bu_ref[:, im], ar, ai, s0r, s0i)
        bu_ref[:, re] = br
        bu_ref[:, im] = bi

    def group(gi, c):
        r0 = pl.multiple_of(gi * SUBLANES, SUBLANES)
        rws = pl.ds(r0, SUBLANES)
        sr, si = bu_ref[rws, re], bu_ref[rws, im]
        for k in range(nlog):
            ar, ai = coef_ref[k, :, re], coef_ref[k, :, im]
            pr = pltpu.roll(sr, 1 << k, axis=0)
            pi = pltpu.roll(si, 1 << k, axis=0)
            sr, si = _cmul_add(sr, si, ar, ai, pr, pi)
        if carry:
            cr, ci = c
            pwr, pwi = coef_ref[nlog, :, re], coef_ref[nlog, :, im]
            a8r, a8i = coef_ref[nlog + 1, :, re], coef_ref[nlog + 1, :, im]
            lr = jnp.broadcast_to(sr[SUBLANES - 1:SUBLANES, :], sr.shape)
            li = jnp.broadcast_to(si[SUBLANES - 1:SUBLANES, :], si.shape)
            sr, si = _cmul_add(sr, si, pwr, pwi, cr, ci)
            c = _cmul_add(lr, li, a8r, a8i, cr, ci)
        bu_ref[rws, re] = sr
        bu_ref[rws, im] = si
        return c

    zero = jnp.zeros((SUBLANES, ns), F32)
    c = lax.fori_loop(0, rows // SUBLANES, group, (zero, zero))
    if carry:
        st_ref[:, re] = c[0][0:1, :]
        st_ref[:, im] = c[1][0:1, :]
    else:
        st_ref[...] = bu_ref[...]

    dvec = d_ref[...]
    for rb in range(nblk):
        r = slice(rb * row_block, (rb + 1) * row_block)
        y = _dot(bu_ref[r, :].astype(BF16), cblk_ref[...]) + dvec * x_ref[r, :]
        y_ref[r, :] = jax.nn.gelu(y).astype(y_ref.dtype)


def _s5_tables(lam_re, lam_im, log_dt, b_re, b_im, c_re, c_im, seg, with_carry):
    g, p = lam_re.shape
    w = b_re.shape[-1]
    gpc = LANES // w
    nc = g // gpc
    dt = jnp.exp(log_dt)[:, None]
    mag = jnp.exp(lam_re * dt)
    ang = lam_im * dt
    a_re, a_im = mag * jnp.cos(ang), mag * jnp.sin(ang)
    den = lam_re * lam_re + lam_im * lam_im
    coef_re = ((a_re - 1.0) * lam_re + a_im * lam_im) / den
    coef_im = (a_im * lam_re - (a_re - 1.0) * lam_im) / den
    bb_re = coef_re[..., None] * b_re - coef_im[..., None] * b_im
    bb_im = coef_re[..., None] * b_im + coef_im[..., None] * b_re
    eye = jnp.eye(gpc, dtype=F32)

    def blockdiag_in(bb):
        t = bb.reshape(nc, gpc, p, w)
        t = jnp.einsum('cgpw,gh->cgwhp', t, eye)
        return t.reshape(nc, gpc * w, gpc * p)

    def blockdiag_out(cc):
        t = cc.reshape(nc, gpc, w, p)
        t = jnp.einsum('cgwp,gh->cgphw', t, eye)
        return t.reshape(nc, gpc * p, gpc * w)

    bblk = jnp.concatenate([blockdiag_in(bb_re), blockdiag_in(bb_im)], axis=-1).astype(BF16)
    cblk = jnp.concatenate([blockdiag_out(c_re), -blockdiag_out(c_im)], axis=1).astype(BF16)

    def rows_of(zr, zi):
        return jnp.concatenate([zr.reshape(nc, gpc * p), zi.reshape(nc, gpc * p)], axis=-1)

    def csq(zr, zi):
        return zr * zr - zi * zi, 2.0 * zr * zi

    row = jnp.arange(SUBLANES)
    tabs = []
    pr, pi = a_re, a_im
    nlog = int(math.log2(seg))
    for k in range(nlog):
        keep = ((row % seg) >= (1 << k)).astype(F32)[None, :, None]
        tabs.append(rows_of(pr, pi)[:, None, :] * keep)
        pr, pi = csq(pr, pi)
    if with_carry:
        pws = []
        qr, qi = a_re, a_im
        for _ in range(SUBLANES):
            pws.append(rows_of(qr, qi))
            qr, qi = qr * a_re - qi * a_im, qr * a_im + qi * a_re
        tabs.append(jnp.stack(pws, axis=1))
        tabs.append(jnp.broadcast_to(pws[-1][:, None, :], pws[-1].shape[:1] + (SUBLANES,) + pws[-1].shape[1:]))
    else:
        a_row = rows_of(a_re, a_im)
        tabs.append(jnp.broadcast_to(a_row[:, None, :], a_row.shape[:1] + (SUBLANES,) + a_row.shape[1:]))
    coef = jnp.stack(tabs, axis=1)
    return bblk, cblk, coef, nlog


def _s5_core(h3, tables, d_skip2, sj, s0_rows=None):
    bblk, cblk, coef, nlog = tables
    nb, rows, d = h3.shape
    nc = d // LANES
    ns2 = bblk.shape[-1]
    ntab = coef.shape[1]
    has_s0 = s0_rows is not None
    row_block = _pick(rows, (256, 128, 64, 32, 16, 8))
    body = functools.partial(_s5_body, rows=rows, nlog=nlog, carry=not has_s0, has_s0=has_s0,
                             row_block=row_block)
    in_specs = [
        pl.BlockSpec((None, rows, LANES), lambda b, c: (b, 0, c)),
        pl.BlockSpec((None, LANES, ns2), lambda b, c: (c, 0, 0)),
        pl.BlockSpec((None, ns2, LANES), lambda b, c: (c, 0, 0)),
        pl.BlockSpec((None, ntab, SUBLANES, ns2), lambda b, c: (c, 0, 0, 0)),
        pl.BlockSpec((None, 1, LANES), lambda b, c: (sj, 0, c)),
    ]
    args = [h3, bblk, cblk, coef, d_skip2]
    st_rows = rows if has_s0 else 1
    if has_s0:
        in_specs.append(pl.BlockSpec((None, None, rows, ns2), lambda b, c: (b, c, 0, 0)))
        args.append(s0_rows)
    return pl.pallas_call(
        body,
        grid=(nb, nc),
        in_specs=in_specs,
        out_specs=[
            pl.BlockSpec((None, rows, LANES), lambda b, c: (b, 0, c)),
            pl.BlockSpec((None, None, st_rows, ns2), lambda b, c: (b, c, 0, 0)),
        ],
        out_shape=[
            jax.ShapeDtypeStruct((nb, rows, d), BF16),
            jax.ShapeDtypeStruct((nb, nc, st_rows, ns2), F32),
        ],
        scratch_shapes=[pltpu.VMEM((rows, ns2), F32)],
        compiler_params=_params(("parallel", "parallel"), VMEM_LIMIT_SMALL),
        name="s5_core",
    )(*args)


def _subln(o, g, out_scale):
    return o * lax.rsqrt(jnp.mean(o * o, axis=-1, keepdims=True) + LN_EPS) * g * out_scale


def _online_softmax_step(s, m, l, acc_ref, v):
    m_new = jnp.maximum(m, jnp.max(s, axis=-1, keepdims=True))
    corr = jnp.exp(m - m_new)
    p = jnp.exp(s - m_new)
    l_new = l * corr + jnp.sum(p, axis=-1, keepdims=True)
    acc_ref[...] = acc_ref[...] * corr + _dot(p.astype(BF16), v)
    return m_new, l_new


def _attn_prompt_body(slopes_ref, lam_ref, q_ref, k_ref, v_ref, g_ref, o_ref, acc1_ref, acc2_ref, *, tq, half,
                      out_scale):
    h = pl.program_id(1)
    qi = pl.program_id(2)
    neg_slope = -slopes_ref[h]
    lam = lam_ref[0]
    q = q_ref[...]
    lane = lax.broadcasted_iota(jnp.int32, q.shape, 1)
    q1 = jnp.where(lane < half, q, jnp.zeros_like(q))
    q2 = jnp.where(lane >= half, q, jnp.zeros_like(q))
    rel = (lax.broadcasted_iota(jnp.int32, (tq, tq), 0) - lax.broadcasted_iota(jnp.int32, (tq, tq), 1)).astype(F32)
    acc1_ref[...] = jnp.zeros_like(acc1_ref)
    acc2_ref[...] = jnp.zeros_like(acc2_ref)

    def block(kj, carry, masked):
        m1, l1, m2, l2 = carry
        k0 = pl.multiple_of(kj * tq, tq)
        kb = k_ref[pl.ds(k0, tq), :]
        vb = v_ref[pl.ds(k0, tq), :]
        dist = rel + ((qi - kj) * tq).astype(F32)
        bias = dist * neg_slope
        s1 = _dot_nt(q1, kb) + bias
        s2 = _dot_nt(q2, kb) + bias
        if masked:
            s1 = jnp.where(dist >= 0, s1, NEG_INF)
            s2 = jnp.where(dist >= 0, s2, NEG_INF)
        m1, l1 = _online_softmax_step(s1, m1, l1, acc1_ref, vb)
        m2, l2 = _online_softmax_step(s2, m2, l2, acc2_ref, vb)
        return m1, l1, m2, l2

    neg = jnp.full((tq, 1), NEG_INF, F32)
    zero = jnp.zeros((tq, 1), F32)
    carry = lax.fori_loop(0, qi, lambda kj, c: block(kj, c, False), (neg, zero, neg, zero))
    m1, l1, m2, l2 = block(qi, carry, True)
    o = acc1_ref[...] / l1 - lam * (acc2_ref[...] / l2)
    o_ref[...] = _subln(o, g_ref[...], out_scale).astype(o_ref.dtype)


def _attn_prompt(q, k, v, slopes, lam, subln_g3, aj, nb, seq, n_heads, out_scale):
    m, hd = q.shape
    dh = hd // n_heads
    tq = _pick(seq, (256, 128, 64, 32, 16, 8))
    nq = seq // tq
    body = functools.partial(_attn_prompt_body, tq=tq, half=dh // 2, out_scale=out_scale)
    grid_spec = pltpu.PrefetchScalarGridSpec(
        num_scalar_prefetch=0,
        grid=(nb, n_heads, nq),
        in_specs=[
            pl.BlockSpec(memory_space=pltpu.SMEM),
            pl.BlockSpec(memory_space=pltpu.SMEM),
            pl.BlockSpec((tq, dh), lambda b, h, i: (b * nq + i, h)),
            pl.BlockSpec((seq, dh), lambda b, h, i: (b, h)),
            pl.BlockSpec((seq, dh), lambda b, h, i: (b, h)),
            pl.BlockSpec((None, 1, dh), lambda b, h, i: (aj, 0, 0)),
        ],
        out_specs=pl.BlockSpec((tq, dh), lambda b, h, i: (b * nq + i, h)),
        scratch_shapes=[pltpu.VMEM((tq, dh), F32), pltpu.VMEM((tq, dh), F32)],
    )
    return pl.pallas_call(
        body,
        grid_spec=grid_spec,
        out_shape=jax.ShapeDtypeStruct((m, hd), BF16),
        compiler_params=_params(("parallel", "parallel", "arbitrary"), VMEM_LIMIT_SMALL),
        name="diff_attn_prompt",
    )(slopes, lam, q, k, v, subln_g3)


def _attn_decode_body(pt_ref, lam_ref, qbd_ref, k_ref, v_ref, kn_ref, vn_ref, slope_ref, step_ref, g_ref, o_ref,
                      acc_ref, m_ref, l_ref, *, n_pages, page, n_heads, dh, t_new, out_scale):
    p = pl.program_id(1)
    nr = qbd_ref.shape[0]
    past = n_pages * page

    @pl.when(p == 0)
    def _():
        m_ref[...] = jnp.full_like(m_ref, NEG_INF)
        l_ref[...] = jnp.zeros_like(l_ref)
        acc_ref[...] = jnp.zeros_like(acc_ref)

    qbd = qbd_ref[...]
    qpos = step_ref[...] + float(past)
    col = lax.broadcasted_iota(jnp.int32, (nr, page), 1)

    def update(kb, vb, kpos, valid):
        s = _dot_nt(qbd, kb) - slope_ref[...] * (qpos - kpos)
        if valid is not None:
            s = jnp.where(valid, s, NEG_INF)
        m_new, l_new = _online_softmax_step(s, m_ref[...], l_ref[...], acc_ref, vb)
        m_ref[...] = m_new
        l_ref[...] = l_new

    update(k_ref[...].astype(BF16), v_ref[...].astype(BF16), (p * page + col).astype(F32), None)

    @pl.when(p == n_pages - 1)
    def _():
        kpos = (past + col).astype(F32)
        valid = jnp.logical_and(col < t_new, kpos <= qpos)
        update(kn_ref[...], vn_ref[...], kpos, valid)
        lam = lam_ref[0]
        inv = 1.0 / l_ref[...]
        rph = nr // n_heads
        for h in range(n_heads):
            blk = acc_ref[h * rph:(h + 1) * rph, h * dh:(h + 1) * dh] * inv[h * rph:(h + 1) * rph, :]
            o = blk[0:t_new, :] - lam * blk[rph // 2:rph // 2 + t_new, :]
            o_ref[:, h * dh:(h + 1) * dh] = _subln(o, g_ref[...], out_scale)


def _attn_decode(q, k_new, v_new, cache_k4, cache_v4, page_table, slopes, lam, subln_g3, aj, nb, t_new, n_heads,
                 out_scale):
    hd = q.shape[1]
    dh = hd // n_heads
    half = dh // 2
    page = cache_k4.shape[2]
    n_pages = page_table.shape[1]
    rph = SUBLANES
    assert t_new <= rph // 2
    nr = n_heads * rph
    q5 = q.reshape(nb, t_new, n_heads, 2, half)
    q5 = jnp.pad(q5, ((0, 0), (0, rph // 2 - t_new), (0, 0), (0, 0), (0, 0)))
    qr = jnp.transpose(q5, (0, 2, 3, 1, 4))
    qbd = jnp.einsum('bhmtd,hg,mn->bhmtgnd', qr, jnp.eye(n_heads, dtype=q.dtype), jnp.eye(2, dtype=q.dtype))
    qbd = qbd.reshape(nb, nr, hd)
    pad_rows = ((0, 0), (0, page - t_new), (0, 0))
    kn = jnp.pad(k_new.reshape(nb, t_new, hd), pad_rows)
    vn = jnp.pad(v_new.reshape(nb, t_new, hd), pad_rows)
    slope_rows = jnp.repeat(slopes, rph)[:, None]
    step_rows = jnp.tile(jnp.arange(rph // 2, dtype=F32), 2 * n_heads)[:, None]
    body = functools.partial(_attn_decode_body, n_pages=n_pages, page=page, n_heads=n_heads, dh=dh, t_new=t_new,
                             out_scale=out_scale)
    grid_spec = pltpu.PrefetchScalarGridSpec(
        num_scalar_prefetch=1,
        grid=(nb, n_pages),
        in_specs=[
            pl.BlockSpec(memory_space=pltpu.SMEM),
            pl.BlockSpec((None, nr, hd), lambda b, p, pt: (b, 0, 0)),
            pl.BlockSpec((None, None, page, hd), lambda b, p, pt: (aj, pt[b * n_pages + p], 0, 0)),
            pl.BlockSpec((None, None, page, hd), lambda b, p, pt: (aj, pt[b * n_pages + p], 0, 0)),
            pl.BlockSpec((None, page, hd), lambda b, p, pt: (b, 0, 0)),
            pl.BlockSpec((None, page, hd), lambda b, p, pt: (b, 0, 0)),
            pl.BlockSpec((nr, 1), lambda b, p, pt: (0, 0)),
            pl.BlockSpec((nr, 1), lambda b, p, pt: (0, 0)),
            pl.BlockSpec((None, 1, dh), lambda b, p, pt: (aj, 0, 0)),
        ],
        out_specs=pl.BlockSpec((None, t_new, hd), lambda b, p, pt: (b, 0, 0)),
        scratch_shapes=[pltpu.VMEM((nr, hd), F32), pltpu.VMEM((nr, 1), F32), pltpu.VMEM((nr, 1), F32)],
    )
    return pl.pallas_call(
        body,
        grid_spec=grid_spec,
        out_shape=jax.ShapeDtypeStruct((nb, t_new, hd), F32),
        compiler_params=_params(("parallel", "arbitrary"), VMEM_LIMIT_SMALL),
        name="diff_attn_decode",
    )(page_table.reshape(-1), lam, qbd, cache_k4, cache_v4, kn, vn, slope_rows, step_rows, subln_g3)


def kernel(x_prompt, x_sample, cache_k, cache_v, state_s5_re, state_s5_im, page_table, p_prompt, p_sample,
           ln_g, ln_b, w_ffn_gate, w_ffn_up, w_ffn_down, w_pe, w_pe_gate,
           s5_lambda_re, s5_lambda_im, s5_log_dt, s5_b_re, s5_b_im, s5_c_re, s5_c_im, s5_d,
           s5_glu_w, s5_glu_b, s5_w_out,
           attn_w_qkv, attn_lambda_q1, attn_lambda_k1, attn_lambda_q2, attn_lambda_k2, attn_subln_g, attn_w_o):
    depth = ln_g.shape[0]
    alpha = (2.0 * depth) ** 0.25
    nbp, seq, d = x_prompt.shape
    nbs, t_new, _ = x_sample.shape
    n_attn, n_pool, page, n_heads, qk_dim = cache_k.shape
    v_dim = cache_v.shape[-1]
    half = qk_dim // 2
    attn_scale = 1.0 / math.sqrt(half)
    g_cnt, p_cnt = s5_lambda_re.shape[1:]
    gw = s5_b_re.shape[-1]
    gpc = LANES // gw
    nc = d // LANES
    hq = n_heads * qk_dim

    ln_g4 = ln_g.reshape(depth, ln_g.shape[1], 1, d)
    ln_b4 = ln_b.reshape(depth, ln_b.shape[1], 1, d)
    pp3 = p_prompt.reshape(depth, nbp * seq, -1)
    ps3 = p_sample.reshape(depth, nbs * t_new, -1)
    cache_k4 = cache_k.reshape(n_attn, n_pool, page, n_heads * qk_dim)
    cache_v4 = cache_v.reshape(n_attn, n_pool, page, n_heads * v_dim)
    glu_b3 = s5_glu_b.reshape(s5_glu_b.shape[0], 1, d)
    d_skip3 = s5_d.reshape(s5_d.shape[0], 1, d)
    subln_g3 = attn_subln_g.reshape(n_attn, 1, v_dim)
    slopes = jnp.exp2(-8.0 * jnp.arange(1, n_heads + 1, dtype=F32) / n_heads)

    hp = x_prompt.reshape(nbp * seq, d)
    hs = x_sample.reshape(nbs * t_new, d)
    kp, vp, ksm, vsm = [], [], [], []
    srp, sip, srs, sis = [], [], [], []

    def split_state(st):
        lead = st.shape[:-2]
        st = st.reshape(lead + (nc, 2, gpc, p_cnt))
        return (st[..., 0, :, :].reshape(lead + (g_cnt, p_cnt)), st[..., 1, :, :].reshape(lead + (g_cnt, p_cnt)))

    for i in range(depth):
        j = i // 2
        hp = _ffn_ln(hp, w_ffn_gate, w_ffn_up, w_ffn_down, ln_g4, ln_b4, i, 0, 0, alpha)
        hs = _ffn_ln(hs, w_ffn_gate, w_ffn_up, w_ffn_down, ln_g4, ln_b4, i, 0, 0, alpha)
        if i % 2 == 0:
            prm = (s5_lambda_re[j], s5_lambda_im[j], s5_log_dt[j], s5_b_re[j], s5_b_im[j], s5_c_re[j], s5_c_im[j])
            tabs_p = _s5_tables(*prm, seg=SUBLANES, with_carry=True)
            yp, stp = _s5_core(hp.reshape(nbp, seq, d), tabs_p, d_skip3, j)
            re, im = split_state(stp[:, :, 0, :])
            srp.append(re)
            sip.append(im)
            hp = _glu_out_ln(yp.reshape(nbp * seq, d), hp, s5_glu_w, glu_b3, s5_w_out, ln_g4, ln_b4, j, i, 1, alpha)
            tabs_s = _s5_tables(*prm, seg=t_new, with_carry=False)
            s0 = jnp.concatenate([state_s5_re[j].reshape(nbs, nc, gpc * p_cnt),
                                  state_s5_im[j].reshape(nbs, nc, gpc * p_cnt)], axis=-1)
            s0_rows = jnp.zeros((nbs, t_new, nc, s0.shape[-1]), F32).at[:, 0].set(s0)
            s0_rows = jnp.transpose(s0_rows.reshape(nbs * t_new, nc, -1), (1, 0, 2))[None]
            ys, sts = _s5_core(hs.reshape(1, nbs * t_new, d), tabs_s, d_skip3, j, s0_rows)
            st_last = sts[0].reshape(nc, nbs, t_new, -1)[:, :, t_new - 1, :]
            re, im = split_state(jnp.transpose(st_last, (1, 0, 2)))
            srs.append(re)
            sis.append(im)
            hs = _glu_out_ln(ys.reshape(nbs * t_new, d), hs, s5_glu_w, glu_b3, s5_w_out, ln_g4, ln_b4, j, i, 1,
                             alpha)
        else:
            lam_init = 0.8 - 0.6 * math.exp(-0.3 * i)
            lam = (jnp.exp(jnp.sum(attn_lambda_q1[j] * attn_lambda_k1[j]))
                   - jnp.exp(jnp.sum(attn_lambda_q2[j] * attn_lambda_k2[j])) + lam_init).reshape(1).astype(F32)
            out_scale = 1.0 - lam_init
            (qp,) = _proj(hp, attn_w_qkv, j, 0, hq, (BF16,), scale=attn_scale)
            kf, kb = _proj(hp, attn_w_qkv, j, hq, hq, (F32, BF16))
            vf, vb = _proj(hp, attn_w_qkv, j, 2 * hq, n_heads * v_dim, (F32, BF16))
            kp.append(kf.reshape(nbp, seq, n_heads, qk_dim))
            vp.append(vf.reshape(nbp, seq, n_heads, v_dim))
            op = _attn_prompt(qp, kb, vb, slopes, lam, subln_g3, j, nbp, seq, n_heads, out_scale)
            hp = _proj_ln(op, hp, attn_w_o, j, ln_g4, ln_b4, i, 1, alpha)
            (qs,) = _proj(hs, attn_w_qkv, j, 0, hq, (BF16,), scale=attn_scale)
            kf, kb = _proj(hs, attn_w_qkv, j, hq, hq, (F32, BF16))
            vf, vb = _proj(hs, attn_w_qkv, j, 2 * hq, n_heads * v_dim, (F32, BF16))
            ksm.append(kf.reshape(nbs, t_new, n_heads, qk_dim))
            vsm.append(vf.reshape(nbs, t_new, n_heads, v_dim))
            osm = _attn_decode(qs, kb, vb, cache_k4, cache_v4, page_table, slopes, lam, subln_g3, j, nbs, t_new,
                               n_heads, out_scale)
            hs = _proj_ln(osm.reshape(nbs * t_new, n_heads * v_dim), hs, attn_w_o, j, ln_g4, ln_b4, i, 1, alpha)
        hp = _ffn_ln(hp, w_ffn_gate, w_ffn_up, w_ffn_down, ln_g4, ln_b4, i, 1, 2, alpha)
        hs = _ffn_ln(hs, w_ffn_gate, w_ffn_up, w_ffn_down, ln_g4, ln_b4, i, 1, 2, alpha)
        hp = _pe_gate(hp, pp3, w_pe, w_pe_gate, i)
        hs = _pe_gate(hs, ps3, w_pe, w_pe_gate, i)

    return (hp.reshape(nbp, seq, d), hs.reshape(nbs, t_new, d),
            jnp.stack(kp), jnp.stack(vp), jnp.stack(ksm), jnp.stack(vsm),
            jnp.stack(srp), jnp.stack(sip), jnp.stack(srs), jnp.stack(sis))
```

```python
import functools
import math

import jax
import jax.numpy as jnp
from jax import lax
from jax.experimental import pallas as pl
from jax.experimental.pallas import tpu as pltpu

F32 = jnp.float32
BF16 = jnp.bfloat16
LN_EPS = 1e-5
NEG_INF = -1e30
LANES = 128
SUBLANES = 8
VMEM_LIMIT_BIG = 56 * 1024 * 1024
VMEM_LIMIT_SMALL = 40 * 1024 * 1024
EPILOGUE_ROWS = 128


def _pick(n, cands):
    for c in cands:
        if n % c == 0:
            return c
    return n


def _row_tile(m):
    return _pick(m, (1024, 512, 256, 128, 64, 32, 16, 8))


def _params(sem, vmem=VMEM_LIMIT_BIG):
    return pltpu.CompilerParams(dimension_semantics=sem, vmem_limit_bytes=vmem)


def _dot(a, b):
    return jnp.dot(a, b, preferred_element_type=F32)


def _dot_nt(a, b):
    return lax.dot_general(a, b, (((1,), (1,)), ((), ())), preferred_element_type=F32)


def _layer_norm(z, g, b):
    mu = jnp.mean(z, axis=-1, keepdims=True)
    zc = z - mu
    var = jnp.mean(zc * zc, axis=-1, keepdims=True)
    return zc * lax.rsqrt(var + LN_EPS) * g + b


def _ln_epilogue(o_ref, res_ref, g_ref, b_ref, alpha, scale):
    tm = o_ref.shape[0]
    rb = min(EPILOGUE_ROWS, tm)
    g = g_ref[...]
    b = b_ref[...]

    def chunk(r, carry):
        rows = pl.ds(pl.multiple_of(r * rb, rb), rb)
        z = alpha * res_ref[rows, :].astype(F32) + scale * o_ref[rows, :]
        o_ref[rows, :] = _layer_norm(z, g, b)
        return carry

    lax.fori_loop(0, tm // rb, chunk, 0)


def _ffn_ln_body(x_ref, wg_ref, wu_ref, wd_ref, g_ref, b_ref, o_ref, xb_ref, *, alpha, nf, col_chunk):
    j = pl.program_id(1)

    @pl.when(j == 0)
    def _():
        xb_ref[...] = x_ref[...].astype(BF16)
        o_ref[...] = jnp.zeros_like(o_ref)

    xb = xb_ref[...]
    gate = _dot(xb, wg_ref[...].astype(BF16))
    up = _dot(xb, wu_ref[...].astype(BF16))
    h = (gate * jax.nn.sigmoid(gate) * up).astype(BF16)
    d = o_ref.shape[1]
    for c in range(d // col_chunk):
        cols = slice(c * col_chunk, (c + 1) * col_chunk)
        o_ref[:, cols] += _dot(h, wd_ref[:, cols].astype(BF16))

    @pl.when(j == nf - 1)
    def _():
        _ln_epilogue(o_ref, x_ref, g_ref, b_ref, alpha, 0.5)


def _ffn_ln(x, w_gate, w_up, w_down, ln_g4, ln_b4, li, si, ln_idx, alpha):
    m, d = x.shape
    f = w_gate.shape[-1]
    tm = _row_tile(m)
    tf = _pick(f, (256, 128)) if tm >= 512 else _pick(f, (512, 256, 128))
    nf = f // tf
    body = functools.partial(_ffn_ln_body, alpha=alpha, nf=nf, col_chunk=_pick(d, (512, 256, 128)))
    return pl.pallas_call(
        body,
        grid=(m // tm, nf),
        in_specs=[
            pl.BlockSpec((tm, d), lambda i, j: (i, 0)),
            pl.BlockSpec((None, None, d, tf), lambda i, j: (li, si, 0, j)),
            pl.BlockSpec((None, None, d, tf), lambda i, j: (li, si, 0, j)),
            pl.BlockSpec((None, None, tf, d), lambda i, j: (li, si, j, 0)),
            pl.BlockSpec((None, None, 1, d), lambda i, j: (li, ln_idx, 0, 0)),
            pl.BlockSpec((None, None, 1, d), lambda i, j: (li, ln_idx, 0, 0)),
        ],
        out_specs=pl.BlockSpec((tm, d), lambda i, j: (i, 0)),
        out_shape=jax.ShapeDtypeStruct((m, d), F32),
        scratch_shapes=[pltpu.VMEM((tm, d), BF16)],
        compiler_params=_params(("parallel", "arbitrary")),
        name="ffn_ln",
    )(x, w_gate, w_up, w_down, ln_g4, ln_b4)


def _glu_out_ln_body(y_ref, res_ref, wglu_ref, bglu_ref, wout_ref, g_ref, b_ref, o_ref, z_ref, *, alpha, nt, tn):
    j = pl.program_id(1)

    @pl.when(j < nt)
    def _():
        cols = pl.ds(pl.multiple_of(j * tn, tn), tn)
        t = _dot(y_ref[...], wglu_ref[...].astype(BF16)) + bglu_ref[...]
        z_ref[:, cols] = (y_ref[:, cols].astype(F32) * jax.nn.sigmoid(t)).astype(BF16)

    @pl.when(j >= nt)
    def _():
        cols = pl.ds(pl.multiple_of((j - nt) * tn, tn), tn)
        o_ref[:, cols] = _dot(z_ref[...], wout_ref[...].astype(BF16))

    @pl.when(j == 2 * nt - 1)
    def _():
        _ln_epilogue(o_ref, res_ref, g_ref, b_ref, alpha, 1.0)


def _glu_out_ln(y, res, glu_w, glu_b3, w_out, ln_g4, ln_b4, sj, li, ln_idx, alpha):
    m, d = res.shape
    tm = _pick(m, (512, 256, 128, 64, 32, 16, 8))
    tn = _pick(d, (512, 256, 128))
    nt = d // tn
    body = functools.partial(_glu_out_ln_body, alpha=alpha, nt=nt, tn=tn)
    return pl.pallas_call(
        body,
        grid=(m // tm, 2 * nt),
        in_specs=[
            pl.BlockSpec((tm, d), lambda i, j: (i, 0)),
            pl.BlockSpec((tm, d), lambda i, j: (i, 0)),
            pl.BlockSpec((None, d, tn), lambda i, j: (sj, 0, jnp.minimum(j, nt - 1))),
            pl.BlockSpec((None, 1, tn), lambda i, j: (sj, 0, jnp.minimum(j, nt - 1))),
            pl.BlockSpec((None, d, tn), lambda i, j: (sj, 0, jnp.maximum(j - nt, 0))),
            pl.BlockSpec((None, None, 1, d), lambda i, j: (li, ln_idx, 0, 0)),
            pl.BlockSpec((None, None, 1, d), lambda i, j: (li, ln_idx, 0, 0)),
        ],
        out_specs=pl.BlockSpec((tm, d), lambda i, j: (i, 0)),
        out_shape=jax.ShapeDtypeStruct((m, d), F32),
        scratch_shapes=[pltpu.VMEM((tm, d), BF16)],
        compiler_params=_params(("parallel", "arbitrary")),
        name="s5_glu_out_ln",
    )(y, res, glu_w, glu_b3, w_out, ln_g4, ln_b4)


def _qkv_body(x_ref, w_ref, q_ref, kf_ref, kb_ref, vf_ref, vb_ref, xb_ref, *, scale, nq, nk):
    j = pl.program_id(1)

    @pl.when(j == 0)
    def _():
        xb_ref[...] = x_ref[...].astype(BF16)

    r = _dot(xb_ref[...], w_ref[...].astype(BF16))

    @pl.when(j < nq)
    def _():
        q_ref[...] = (r * scale).astype(q_ref.dtype)

    @pl.when(jnp.logical_and(j >= nq, j < nq + nk))
    def _():
        kf_ref[...] = r
        kb_ref[...] = r.astype(kb_ref.dtype)

    @pl.when(j >= nq + nk)
    def _():
        vf_ref[...] = r
        vb_ref[...] = r.astype(vb_ref.dtype)


def _qkv_proj(x, w3, wj, n_q, n_k, n_v, scale):
    m, d = x.shape
    tm = _row_tile(m)
    tn = _pick(math.gcd(n_q, math.gcd(n_k, n_v)), (512, 256, 128))
    nq, nk, nv = n_q // tn, n_k // tn, n_v // tn
    body = functools.partial(_qkv_body, scale=scale, nq=nq, nk=nk)
    q_map = lambda i, j: (i, jnp.minimum(j, nq - 1))
    k_map = lambda i, j: (i, jnp.clip(j - nq, 0, nk - 1))
    v_map = lambda i, j: (i, jnp.maximum(j - nq - nk, 0))
    return pl.pallas_call(
        body,
        grid=(m // tm, nq + nk + nv),
        in_specs=[
            pl.BlockSpec((tm, d), lambda i, j: (i, 0)),
            pl.BlockSpec((None, d, tn), lambda i, j: (wj, 0, j)),
        ],
        out_specs=[
            pl.BlockSpec((tm, tn), q_map),
            pl.BlockSpec((tm, tn), k_map),
            pl.BlockSpec((tm, tn), k_map),
            pl.BlockSpec((tm, tn), v_map),
            pl.BlockSpec((tm, tn), v_map),
        ],
        out_shape=[
            jax.ShapeDtypeStruct((m, n_q), BF16),
            jax.ShapeDtypeStruct((m, n_k), F32),
            jax.ShapeDtypeStruct((m, n_k), BF16),
            jax.ShapeDtypeStruct((m, n_v), F32),
            jax.ShapeDtypeStruct((m, n_v), BF16),
        ],
        scratch_shapes=[pltpu.VMEM((tm, d), BF16)],
        compiler_params=_params(("parallel", "arbitrary")),
        name="qkv_proj",
    )(x, w3)


def _proj_ln_body(x_ref, res_ref, w_ref, g_ref, b_ref, o_ref, *, alpha, nn, tn):
    j = pl.program_id(1)
    o_ref[:, pl.ds(pl.multiple_of(j * tn, tn), tn)] = _dot(x_ref[...].astype(BF16), w_ref[...].astype(BF16))

    @pl.when(j == nn - 1)
    def _():
        _ln_epilogue(o_ref, res_ref, g_ref, b_ref, alpha, 1.0)


def _proj_ln(x, res, w3, wj, ln_g4, ln_b4, li, ln_idx, alpha):
    m, k = x.shape
    d = res.shape[1]
    tm = _row_tile(m)
    tn = _pick(d, (512, 256, 128))
    nn = d // tn
    body = functools.partial(_proj_ln_body, alpha=alpha, nn=nn, tn=tn)
    return pl.pallas_call(
        body,
        grid=(m // tm, nn),
        in_specs=[
            pl.BlockSpec((tm, k), lambda i, j: (i, 0)),
            pl.BlockSpec((tm, d), lambda i, j: (i, 0)),
            pl.BlockSpec((None, k, tn), lambda i, j: (wj, 0, j)),
            pl.BlockSpec((None, None, 1, d), lambda i, j: (li, ln_idx, 0, 0)),
            pl.BlockSpec((None, None, 1, d), lambda i, j: (li, ln_idx, 0, 0)),
        ],
        out_specs=pl.BlockSpec((tm, d), lambda i, j: (i, 0)),
        out_shape=jax.ShapeDtypeStruct((m, d), F32),
        compiler_params=_params(("parallel", "arbitrary")),
        name="proj_ln",
    )(x, res, w3, ln_g4, ln_b4)


def _pe_body(h_ref, p_ref, wg_ref, wp_ref, o_ref, hb_ref, *, tn):
    j = pl.program_id(1)

    @pl.when(j == 0)
    def _():
        hb_ref[...] = h_ref[...].astype(BF16)

    gate = _dot(hb_ref[...], wg_ref[...].astype(BF16))
    pe = _dot(p_ref[...].astype(BF16), wp_ref[...].astype(BF16))
    hcols = h_ref[:, pl.ds(pl.multiple_of(j * tn, tn), tn)]
    o_ref[...] = hcols + jax.nn.sigmoid(gate) * pe


def _pe_gate(h, p3, w_pe, w_pe_gate, li):
    m, d = h.shape
    pd = p3.shape[-1]
    tm = _row_tile(m)
    tn = _pick(d, (512, 256, 128))
    body = functools.partial(_pe_body, tn=tn)
    return pl.pallas_call(
        body,
        grid=(m // tm, d // tn),
        in_specs=[
            pl.BlockSpec((tm, d), lambda i, j: (i, 0)),
            pl.BlockSpec((None, tm, pd), lambda i, j: (li, i, 0)),
            pl.BlockSpec((None, d, tn), lambda i, j: (li, 0, j)),
            pl.BlockSpec((None, pd, tn), lambda i, j: (li, 0, j)),
        ],
        out_specs=pl.BlockSpec((tm, tn), lambda i, j: (i, j)),
        out_shape=jax.ShapeDtypeStruct((m, d), F32),
        scratch_shapes=[pltpu.VMEM((tm, d), BF16)],
        compiler_params=_params(("parallel", "arbitrary")),
        name="pe_gate",
    )(h, p3, w_pe_gate, w_pe)


def _cmul_add(sr, si, ar, ai, pr, pi):
    return sr + ar * pr - ai * pi, si + ar * pi + ai * pr


def _s5_body(*refs, rows, nlog, carry, has_s0, row_block):
    if has_s0:
        x_ref, bblk_ref, cblk_ref, coef_ref, d_ref, s0_ref, y_ref, st_ref, bu_ref = refs
    else:
        x_ref, bblk_ref, cblk_ref, coef_ref, d_ref, y_ref, st_ref, bu_ref = refs
    ns = bu_ref.shape[1] // 2
    re, im = slice(0, ns), slice(ns, 2 * ns)
    nblk = rows // row_block

    for rb in range(nblk):
        r = slice(rb * row_block, (rb + 1) * row_block)
        bu_ref[r, :] = _dot(x_ref[r, :].astype(BF16), bblk_ref[...])

    if has_s0:
        ar, ai = coef_ref[nlog, 0:1, re], coef_ref[nlog, 0:1, im]
        s0r, s0i = s0_ref[:, re], s0_ref[:, im]
        br, bi = _cmul_add(bu_ref[:, re], bu_ref[:, im], ar, ai, s0r, s0i)
        bu_ref[:, re] = br
        bu_ref[:, im] = bi

    def group(gi, c):
        r0 = pl.multiple_of(gi * SUBLANES, SUBLANES)
        rws = pl.ds(r0, SUBLANES)
        sr, si = bu_ref[rws, re], bu_ref[rws, im]
        for k in range(nlog):
            ar, ai = coef_ref[k, :, re], coef_ref[k, :, im]
            pr = pltpu.roll(sr, 1 << k, axis=0)
            pi = pltpu.roll(si, 1 << k, axis=0)
            sr, si = _cmul_add(sr, si, ar, ai, pr, pi)
        if carry:
            cr, ci = c
            pwr, pwi = coef_ref[nlog, :, re], coef_ref[nlog, :, im]
            a8r, a8i = coef_ref[nlog + 1, :, re], coef_ref[nlog + 1, :, im]
            lr = jnp.broadcast_to(sr[SUBLANES - 1:SUBLANES, :], sr.shape)
            li = jnp.broadcast_to(si[SUBLANES - 1:SUBLANES, :], si.shape)
            sr, si = _cmul_add(sr, si, pwr, pwi, cr, ci)
            c = _cmul_add(lr, li, a8r, a8i, cr, ci)
        bu_ref[rws, re] = sr
        bu_ref[rws, im] = si
        return c

    zero = jnp.zeros((SUBLANES, ns), F32)
    c = lax.fori_loop(0, rows // SUBLANES, group, (zero, zero))
    if carry:
        st_ref[:, re] = c[0][0:1, :]
        st_ref[:, im] = c[1][0:1, :]
    else:
        st_ref[...] = bu_ref[...]

    dvec = d_ref[...]
    for rb in range(nblk):
        r = slice(rb * row_block, (rb + 1) * row_block)
        y = _dot(bu_ref[r, :].astype(BF16), cblk_ref[...]) + dvec * x_ref[r, :]
        y_ref[r, :] = jax.nn.gelu(y).astype(y_ref.dtype)


def _s5_tables(lam_re, lam_im, log_dt, b_re, b_im, c_re, c_im, seg, with_carry):
    g, p = lam_re.shape
    w = b_re.shape[-1]
    gpc = LANES // w
    nc = g // gpc
    dt = jnp.exp(log_dt)[:, None]
    mag = jnp.exp(lam_re * dt)
    ang = lam_im * dt
    a_re, a_im = mag * jnp.cos(ang), mag * jnp.sin(ang)
    den = lam_re * lam_re + lam_im * lam_im
    coef_re = ((a_re - 1.0) * lam_re + a_im * lam_im) / den
    coef_im = (a_im * lam_re - (a_re - 1.0) * lam_im) / den
    bb_re = coef_re[..., None] * b_re - coef_im[..., None] * b_im
    bb_im = coef_re[..., None] * b_im + coef_im[..., None] * b_re
    eye = jnp.eye(gpc, dtype=F32)

    def blockdiag_in(bb):
        t = bb.reshape(nc, gpc, p, w)
        t = jnp.einsum('cgpw,gh->cgwhp', t, eye)
        return t.reshape(nc, gpc * w, gpc * p)

    def blockdiag_out(cc):
        t = cc.reshape(nc, gpc, w, p)
        t = jnp.einsum('cgwp,gh->cgphw', t, eye)
        return t.reshape(nc, gpc * p, gpc * w)

    bblk = jnp.concatenate([blockdiag_in(bb_re), blockdiag_in(bb_im)], axis=-1).astype(BF16)
    cblk = jnp.concatenate([blockdiag_out(c_re), -blockdiag_out(c_im)], axis=1).astype(BF16)

    def rows_of(zr, zi):
        return jnp.concatenate([zr.reshape(nc, gpc * p), zi.reshape(nc, gpc * p)], axis=-1)

    def csq(zr, zi):
        return zr * zr - zi * zi, 2.0 * zr * zi

    row = jnp.arange(SUBLANES)
    tabs = []
    pr, pi = a_re, a_im
    nlog = int(math.log2(seg))
    for k in range(nlog):
        keep = ((row % seg) >= (1 << k)).astype(F32)[None, :, None]
        tabs.append(rows_of(pr, pi)[:, None, :] * keep)
        pr, pi = csq(pr, pi)
    if with_carry:
        pws = []
        qr, qi = a_re, a_im
        for _ in range(SUBLANES):
            pws.append(rows_of(qr, qi))
            qr, qi = qr * a_re - qi * a_im, qr * a_im + qi * a_re
        tabs.append(jnp.stack(pws, axis=1))
        tabs.append(jnp.broadcast_to(pws[-1][:, None, :], pws[-1].shape[:1] + (SUBLANES,) + pws[-1].shape[1:]))
    else:
        a_row = rows_of(a_re, a_im)
        tabs.append(jnp.broadcast_to(a_row[:, None, :], a_row.shape[:1] + (SUBLANES,) + a_row.shape[1:]))
    coef = jnp.stack(tabs, axis=1)
    return bblk, cblk, coef, nlog


def _s5_core(h3, tables, d_skip2, sj, s0_rows=None):
    bblk, cblk, coef, nlog = tables
    nb, rows, d = h3.shape
    nc = d // LANES
    ns2 = bblk.shape[-1]
    ntab = coef.shape[1]
    has_s0 = s0_rows is not None
    row_block = _pick(rows, (256, 128, 64, 32, 16, 8))
    body = functools.partial(_s5_body, rows=rows, nlog=nlog, carry=not has_s0, has_s0=has_s0,
                             row_block=row_block)
    in_specs = [
        pl.BlockSpec((None, rows, LANES), lambda b, c: (b, 0, c)),
        pl.BlockSpec((None, LANES, ns2), lambda b, c: (c, 0, 0)),
        pl.BlockSpec((None, ns2, LANES), lambda b, c: (c, 0, 0)),
        pl.BlockSpec((None, ntab, SUBLANES, ns2), lambda b, c: (c, 0, 0, 0)),
        pl.BlockSpec((None, 1, LANES), lambda b, c: (sj, 0, c)),
    ]
    args = [h3, bblk, cblk, coef, d_skip2]
    st_rows = rows if has_s0 else 1
    if has_s0:
        in_specs.append(pl.BlockSpec((None, None, rows, ns2), lambda b, c: (b, c, 0, 0)))
        args.append(s0_rows)
    return pl.pallas_call(
        body,
        grid=(nb, nc),
        in_specs=in_specs,
        out_specs=[
            pl.BlockSpec((None, rows, LANES), lambda b, c: (b, 0, c)),
            pl.BlockSpec((None, None, st_rows, ns2), lambda b, c: (b, c, 0, 0)),
        ],
        out_shape=[
            jax.ShapeDtypeStruct((nb, rows, d), BF16),
            jax.ShapeDtypeStruct((nb, nc, st_rows, ns2), F32),
        ],
        scratch_shapes=[pltpu.VMEM((rows, ns2), F32)],
        compiler_params=_params(("parallel", "parallel"), VMEM_LIMIT_SMALL),
        name="s5_core",
    )(*args)


def _subln(o, g, out_scale):
    return o * lax.rsqrt(jnp.mean(o * o, axis=-1, keepdims=True) + LN_EPS) * g * out_scale


def _online_softmax_step(s, c, m, l, acc_ref, v):
    m_new = jnp.maximum(m, jnp.max(s, axis=-1, keepdims=True) + c)
    corr = jnp.exp(m - m_new)
    p = jnp.exp(s - (m_new - c))
    l_new = l * corr + jnp.sum(p, axis=-1, keepdims=True)
    acc_ref[...] = acc_ref[...] * corr + _dot(p.astype(BF16), v)
    return m_new, l_new


def _attn_prompt_body(slopes_ref, lam_ref, q_ref, k_ref, v_ref, g_ref, o_ref, acc1_ref, acc2_ref, bias_ref, *, tq,
                      half, out_scale):
    h = pl.program_id(1)
    qi = pl.program_id(2)
    neg_slope = -slopes_ref[h]
    lam = lam_ref[0]
    q = q_ref[...]
    lane = lax.broadcasted_iota(jnp.int32, q.shape, 1)
    q1 = jnp.where(lane < half, q, jnp.zeros_like(q))
    q2 = jnp.where(lane >= half, q, jnp.zeros_like(q))
    rel = (lax.broadcasted_iota(jnp.int32, (tq, tq), 0) - lax.broadcasted_iota(jnp.int32, (tq, tq), 1)).astype(F32)
    bias_ref[0] = rel * neg_slope
    bias_ref[1] = jnp.where(rel >= 0, rel * neg_slope, NEG_INF)
    acc1_ref[...] = jnp.zeros_like(acc1_ref)
    acc2_ref[...] = jnp.zeros_like(acc2_ref)

    def block(kj, carry, diagonal):
        m1, l1, m2, l2 = carry
        k0 = pl.multiple_of(kj * tq, tq)
        kb = k_ref[pl.ds(k0, tq), :]
        vb = v_ref[pl.ds(k0, tq), :]
        c = neg_slope * ((qi - kj) * tq).astype(F32)
        bias = bias_ref[1 if diagonal else 0]
        m1, l1 = _online_softmax_step(_dot_nt(q1, kb) + bias, c, m1, l1, acc1_ref, vb)
        m2, l2 = _online_softmax_step(_dot_nt(q2, kb) + bias, c, m2, l2, acc2_ref, vb)
        return m1, l1, m2, l2

    neg = jnp.full((tq, 1), NEG_INF, F32)
    zero = jnp.zeros((tq, 1), F32)
    carry = lax.fori_loop(0, qi, lambda kj, c: block(kj, c, False), (neg, zero, neg, zero))
    m1, l1, m2, l2 = block(qi, carry, True)
    o = acc1_ref[...] / l1 - lam * (acc2_ref[...] / l2)
    o_ref[...] = _subln(o, g_ref[...], out_scale).astype(o_ref.dtype)


def _attn_prompt(q, k, v, slopes, lam, subln_g3, aj, nb, seq, n_heads, out_scale):
    m, hd = q.shape
    dh = hd // n_heads
    tq = _pick(seq, (512, 256, 128, 64, 32, 16, 8))
    nq = seq // tq
    body = functools.partial(_attn_prompt_body, tq=tq, half=dh // 2, out_scale=out_scale)
    grid_spec = pltpu.PrefetchScalarGridSpec(
        num_scalar_prefetch=0,
        grid=(nb, n_heads, nq),
        in_specs=[
            pl.BlockSpec(memory_space=pltpu.SMEM),
            pl.BlockSpec(memory_space=pltpu.SMEM),
            pl.BlockSpec((tq, dh), lambda b, h, i: (b * nq + i, h)),
            pl.BlockSpec((seq, dh), lambda b, h, i: (b, h)),
            pl.BlockSpec((seq, dh), lambda b, h, i: (b, h)),
            pl.BlockSpec((None, 1, dh), lambda b, h, i: (aj, 0, 0)),
        ],
        out_specs=pl.BlockSpec((tq, dh), lambda b, h, i: (b * nq + i, h)),
        scratch_shapes=[pltpu.VMEM((tq, dh), F32), pltpu.VMEM((tq, dh), F32), pltpu.VMEM((2, tq, tq), F32)],
    )
    return pl.pallas_call(
        body,
        grid_spec=grid_spec,
        out_shape=jax.ShapeDtypeStruct((m, hd), BF16),
        compiler_params=_params(("parallel", "parallel", "arbitrary"), VMEM_LIMIT_SMALL),
        name="diff_attn_prompt",
    )(slopes, lam, q, k, v, subln_g3)


def _attn_decode_body(pt_ref, lam_ref, qc_ref, *refs, n_pages, page, gp, n_heads, t_new, out_scale):
    k_refs, v_refs = refs[:gp], refs[gp:2 * gp]
    (kn_ref, vn_ref, cmat_ref, cnew_ref, slope_ref, step_ref, g_ref, o_ref, acc_ref, m_ref, l_ref,
     s_ref) = refs[2 * gp:]
    p = pl.program_id(1)
    past = n_pages * page

    @pl.when(p == 0)
    def _():
        m_ref[...] = jnp.full_like(m_ref, NEG_INF)
        l_ref[...] = jnp.zeros_like(l_ref)
        acc_ref[...] = jnp.zeros_like(acc_ref)

    qc = qc_ref[...]
    slope = slope_ref[...]
    qpos = step_ref[...] + float(past)

    m_old = m_ref[...]
    m_new = m_old
    offs = []
    for g in range(gp):
        s = _dot_nt(qc, k_refs[g][...].astype(BF16)) + cmat_ref[...]
        s_ref[g] = s
        c = slope * (((p * gp + g) * page).astype(F32) - qpos)
        offs.append(c)
        m_new = jnp.maximum(m_new, jnp.max(s, axis=-1, keepdims=True) + c)
    lsum = jnp.zeros_like(m_new)
    pv = jnp.zeros(acc_ref.shape, F32)
    for g in range(gp):
        pe = jnp.exp(s_ref[g] - (m_new - offs[g]))
        lsum = lsum + jnp.sum(pe, axis=-1, keepdims=True)
        pv = pv + _dot(pe.astype(BF16), v_refs[g][...].astype(BF16))
    corr = jnp.exp(m_old - m_new)
    m_ref[...] = m_new
    l_ref[...] = l_ref[...] * corr + lsum
    acc_ref[...] = acc_ref[...] * corr + pv

    @pl.when(p == n_pages // gp - 1)
    def _():
        s = _dot_nt(qc, kn_ref[...]) + cnew_ref[...]
        m_fin, l_fin = _online_softmax_step(s, 0.0, m_ref[...], l_ref[...], acc_ref, vn_ref[...])
        lam = lam_ref[0]
        inv = 1.0 / l_fin
        rph = acc_ref.shape[0] // n_heads
        dh = acc_ref.shape[1]
        for h in range(n_heads):
            blk = acc_ref[h * rph:(h + 1) * rph, :] * inv[h * rph:(h + 1) * rph, :]
            o = blk[0:t_new, :] - lam * blk[rph // 2:rph // 2 + t_new, :]
            o_ref[:, h * dh:(h + 1) * dh] = _subln(o, g_ref[...], out_scale)


def _attn_decode(q, k_new, v_new, cache_k4, cache_v4, page_table, slopes, lam, subln_g3, aj, nb, t_new, n_heads,
                 out_scale):
    hd = q.shape[1]
    dh = hd // n_heads
    half = dh // 2
    prows = cache_k4.shape[2]
    page = prows // n_heads
    n_pages = page_table.shape[1]
    gp = _pick(n_pages, (4, 2, 1))
    rph = SUBLANES
    tslots = SUBLANES
    assert t_new <= rph // 2 and t_new <= tslots
    nr = n_heads * rph
    q4 = jnp.transpose(q.reshape(nb, t_new, n_heads, dh), (0, 2, 1, 3))
    q4 = jnp.pad(q4, ((0, 0), (0, 0), (0, rph // 2 - t_new), (0, 0)))
    lane = jnp.arange(dh)
    qc = jnp.stack([jnp.where(lane < half, q4, 0), jnp.where(lane >= half, q4, 0)], axis=2).reshape(nb, nr, dh)
    pad_tok = ((0, 0), (0, tslots - t_new), (0, 0), (0, 0))
    kn = jnp.pad(k_new.reshape(nb, t_new, n_heads, dh), pad_tok).reshape(nb, tslots * n_heads, dh)
    vn = jnp.pad(v_new.reshape(nb, t_new, n_heads, dh), pad_tok).reshape(nb, tslots * n_heads, dh)
    row = jnp.arange(nr)
    row_head, row_step = row // rph, (row % (rph // 2)).astype(F32)
    slope_rows = slopes[row_head][:, None]
    step_rows = row_step[:, None]
    col = jnp.arange(prows)
    cmat = jnp.where((col % n_heads)[None, :] == row_head[:, None],
                     slope_rows * (col // n_heads).astype(F32)[None, :], NEG_INF)
    coln = jnp.arange(tslots * n_heads)
    tokn = (coln // n_heads).astype(F32)[None, :]
    ok = jnp.logical_and((coln % n_heads)[None, :] == row_head[:, None],
                         jnp.logical_and(tokn < t_new, tokn <= step_rows))
    cnew = jnp.where(ok, -slope_rows * (step_rows - tokn), NEG_INF)
    body = functools.partial(_attn_decode_body, n_pages=n_pages, page=page, gp=gp, n_heads=n_heads, t_new=t_new,
                             out_scale=out_scale)

    def page_spec(g):
        return pl.BlockSpec((None, None, prows, dh), lambda b, p, pt: (aj, pt[b * n_pages + p * gp + g], 0, 0))

    grid_spec = pltpu.PrefetchScalarGridSpec(
        num_scalar_prefetch=1,
        grid=(nb, n_pages // gp),
        in_specs=[
            pl.BlockSpec(memory_space=pltpu.SMEM),
            pl.BlockSpec((None, nr, dh), lambda b, p, pt: (b, 0, 0)),
            *[page_spec(g) for g in range(gp)],
            *[page_spec(g) for g in range(gp)],
            pl.BlockSpec((None, tslots * n_heads, dh), lambda b, p, pt: (b, 0, 0)),
            pl.BlockSpec((None, tslots * n_heads, dh), lambda b, p, pt: (b, 0, 0)),
            pl.BlockSpec((nr, prows), lambda b, p, pt: (0, 0)),
            pl.BlockSpec((nr, tslots * n_heads), lambda b, p, pt: (0, 0)),
            pl.BlockSpec((nr, 1), lambda b, p, pt: (0, 0)),
            pl.BlockSpec((nr, 1), lambda b, p, pt: (0, 0)),
            pl.BlockSpec((None, 1, dh), lambda b, p, pt: (aj, 0, 0)),
        ],
        out_specs=pl.BlockSpec((None, t_new, hd), lambda b, p, pt: (b, 0, 0)),
        scratch_shapes=[pltpu.VMEM((nr, dh), F32), pltpu.VMEM((nr, 1), F32), pltpu.VMEM((nr, 1), F32),
                        pltpu.VMEM((gp, nr, prows), F32)],
    )
    return pl.pallas_call(
        body,
        grid_spec=grid_spec,
        out_shape=jax.ShapeDtypeStruct((nb, t_new, hd), F32),
        compiler_params=_params(("parallel", "arbitrary"), VMEM_LIMIT_SMALL),
        name="diff_attn_decode",
    )(page_table.reshape(-1), lam, qc, *([cache_k4] * gp), *([cache_v4] * gp), kn, vn, cmat, cnew, slope_rows,
      step_rows, subln_g3)


def kernel(x_prompt, x_sample, cache_k, cache_v, state_s5_re, state_s5_im, page_table, p_prompt, p_sample,
           ln_g, ln_b, w_ffn_gate, w_ffn_up, w_ffn_down, w_pe, w_pe_gate,
           s5_lambda_re, s5_lambda_im, s5_log_dt, s5_b_re, s5_b_im, s5_c_re, s5_c_im, s5_d,
           s5_glu_w, s5_glu_b, s5_w_out,
           attn_w_qkv, attn_lambda_q1, attn_lambda_k1, attn_lambda_q2, attn_lambda_k2, attn_subln_g, attn_w_o):
    depth = ln_g.shape[0]
    alpha = (2.0 * depth) ** 0.25
    nbp, seq, d = x_prompt.shape
    nbs, t_new, _ = x_sample.shape
    n_attn, n_pool, page, n_heads, qk_dim = cache_k.shape
    v_dim = cache_v.shape[-1]
    half = qk_dim // 2
    attn_scale = 1.0 / math.sqrt(half)
    g_cnt, p_cnt = s5_lambda_re.shape[1:]
    gw = s5_b_re.shape[-1]
    gpc = LANES // gw
    nc = d // LANES
    hq = n_heads * qk_dim

    ln_g4 = ln_g.reshape(depth, ln_g.shape[1], 1, d)
    ln_b4 = ln_b.reshape(depth, ln_b.shape[1], 1, d)
    pp3 = p_prompt.reshape(depth, nbp * seq, -1)
    ps3 = p_sample.reshape(depth, nbs * t_new, -1)
    cache_k4 = cache_k.reshape(n_attn, n_pool, page * n_heads, qk_dim)
    cache_v4 = cache_v.reshape(n_attn, n_pool, page * n_heads, v_dim)
    attn_w_qkv = attn_w_qkv.astype(BF16)
    attn_w_o = attn_w_o.astype(BF16)
    s5_glu_w = s5_glu_w.astype(BF16)
    s5_w_out = s5_w_out.astype(BF16)
    w_pe_gate = w_pe_gate.astype(BF16)
    glu_b3 = s5_glu_b.reshape(s5_glu_b.shape[0], 1, d)
    d_skip3 = s5_d.reshape(s5_d.shape[0], 1, d)
    subln_g3 = attn_subln_g.reshape(n_attn, 1, v_dim)
    slopes = jnp.exp2(-8.0 * jnp.arange(1, n_heads + 1, dtype=F32) / n_heads)

    hp = x_prompt.reshape(nbp * seq, d)
    hs = x_sample.reshape(nbs * t_new, d)
    kp, vp, ksm, vsm = [], [], [], []
    srp, sip, srs, sis = [], [], [], []

    def split_state(st):
        lead = st.shape[:-2]
        st = st.reshape(lead + (nc, 2, gpc, p_cnt))
        return (st[..., 0, :, :].reshape(lead + (g_cnt, p_cnt)), st[..., 1, :, :].reshape(lead + (g_cnt, p_cnt)))

    for i in range(depth):
        j = i // 2
        hp = _ffn_ln(hp, w_ffn_gate, w_ffn_up, w_ffn_down, ln_g4, ln_b4, i, 0, 0, alpha)
        hs = _ffn_ln(hs, w_ffn_gate, w_ffn_up, w_ffn_down, ln_g4, ln_b4, i, 0, 0, alpha)
        if i % 2 == 0:
            prm = (s5_lambda_re[j], s5_lambda_im[j], s5_log_dt[j], s5_b_re[j], s5_b_im[j], s5_c_re[j], s5_c_im[j])
            tabs_p = _s5_tables(*prm, seg=SUBLANES, with_carry=True)
            yp, stp = _s5_core(hp.reshape(nbp, seq, d), tabs_p, d_skip3, j)
            re, im = split_state(stp[:, :, 0, :])
            srp.append(re)
            sip.append(im)
            hp = _glu_out_ln(yp.reshape(nbp * seq, d), hp, s5_glu_w, glu_b3, s5_w_out, ln_g4, ln_b4, j, i, 1, alpha)
            tabs_s = _s5_tables(*prm, seg=t_new, with_carry=False)
            s0 = jnp.concatenate([state_s5_re[j].reshape(nbs, nc, gpc * p_cnt),
                                  state_s5_im[j].reshape(nbs, nc, gpc * p_cnt)], axis=-1)
            s0_rows = jnp.zeros((nbs, t_new, nc, s0.shape[-1]), F32).at[:, 0].set(s0)
            s0_rows = jnp.transpose(s0_rows.reshape(nbs * t_new, nc, -1), (1, 0, 2))[None]
            ys, sts = _s5_core(hs.reshape(1, nbs * t_new, d), tabs_s, d_skip3, j, s0_rows)
            st_last = sts[0].reshape(nc, nbs, t_new, -1)[:, :, t_new - 1, :]
            re, im = split_state(jnp.transpose(st_last, (1, 0, 2)))
            srs.append(re)
            sis.append(im)
            hs = _glu_out_ln(ys.reshape(nbs * t_new, d), hs, s5_glu_w, glu_b3, s5_w_out, ln_g4, ln_b4, j, i, 1,
                             alpha)
        else:
            lam_init = 0.8 - 0.6 * math.exp(-0.3 * i)
            lam = (jnp.exp(jnp.sum(attn_lambda_q1[j] * attn_lambda_k1[j]))
                   - jnp.exp(jnp.sum(attn_lambda_q2[j] * attn_lambda_k2[j])) + lam_init).reshape(1).astype(F32)
            out_scale = 1.0 - lam_init
            qp, kf, kb, vf, vb = _qkv_proj(hp, attn_w_qkv, j, hq, hq, n_heads * v_dim, attn_scale)
            kp.append(kf.reshape(nbp, seq, n_heads, qk_dim))
            vp.append(vf.reshape(nbp, seq, n_heads, v_dim))
            op = _attn_prompt(qp, kb, vb, slopes, lam, subln_g3, j, nbp, seq, n_heads, out_scale)
            hp = _proj_ln(op, hp, attn_w_o, j, ln_g4, ln_b4, i, 1, alpha)
            qs, kf, kb, vf, vb = _qkv_proj(hs, attn_w_qkv, j, hq, hq, n_heads * v_dim, attn_scale)
            ksm.append(kf.reshape(nbs, t_new, n_heads, qk_dim))
            vsm.append(vf.reshape(nbs, t_new, n_heads, v_dim))
            osm = _attn_decode(qs, kb, vb, cache_k4, cache_v4, page_table, slopes, lam, subln_g3, j, nbs, t_new,
                               n_heads, out_scale)
            hs = _proj_ln(osm.reshape(nbs * t_new, n_heads * v_dim), hs, attn_w_o, j, ln_g4, ln_b4, i, 1, alpha)
        hp = _ffn_ln(hp, w_ffn_gate, w_ffn_up, w_ffn_down, ln_g4, ln_b4, i, 1, 2, alpha)
        hs = _ffn_ln(hs, w_ffn_gate, w_ffn_up, w_ffn_down, ln_g4, ln_b4, i, 1, 2, alpha)
        hp = _pe_gate(hp, pp3, w_pe, w_pe_gate, i)
        hs = _pe_gate(hs, ps3, w_pe, w_pe_gate, i)

    return (hp.reshape(nbp, seq, d), hs.reshape(nbs, t_new, d),
            jnp.stack(kp), jnp.stack(vp), jnp.stack(ksm), jnp.stack(vsm),
            jnp.stack(srp), jnp.stack(sip), jnp.stack(srs), jnp.stack(sis))
```

```python
import functools
import math

import jax
import jax.numpy as jnp
from jax import lax
from jax.experimental import pallas as pl
from jax.experimental.pallas import tpu as pltpu

F32 = jnp.float32
BF16 = jnp.bfloat16
LN_EPS = 1e-5
NEG_INF = -1e30
LOG2_E = math.log2(math.e)
LANES = 128
SUBLANES = 8
VMEM_LIMIT_BIG = 56 * 1024 * 1024
VMEM_LIMIT_SMALL = 40 * 1024 * 1024
EPILOGUE_ROWS = 128


def _pick(n, cands):
    for c in cands:
        if n % c == 0:
            return c
    return n


def _row_tile(m):
    return _pick(m, (1024, 512, 256, 128, 64, 32, 16, 8))


def _params(sem, vmem=VMEM_LIMIT_BIG):
    return pltpu.CompilerParams(dimension_semantics=sem, vmem_limit_bytes=vmem)


def _dot(a, b):
    return jnp.dot(a, b, preferred_element_type=F32)


def _dot_nt(a, b):
    return lax.dot_general(a, b, (((1,), (1,)), ((), ())), preferred_element_type=F32)


def _layer_norm(z, g, b):
    mu = jnp.mean(z, axis=-1, keepdims=True)
    zc = z - mu
    var = jnp.mean(zc * zc, axis=-1, keepdims=True)
    return zc * lax.rsqrt(var + LN_EPS) * g + b


def _ln_epilogue(o_ref, res_ref, g_ref, b_ref, alpha, scale):
    tm = o_ref.shape[0]
    rb = min(EPILOGUE_ROWS, tm)
    g = g_ref[...]
    b = b_ref[...]

    def chunk(r, carry):
        rows = pl.ds(pl.multiple_of(r * rb, rb), rb)
        z = alpha * res_ref[rows, :].astype(F32) + scale * o_ref[rows, :]
        o_ref[rows, :] = _layer_norm(z, g, b)
        return carry

    lax.fori_loop(0, tm // rb, chunk, 0)


def _ffn_ln_body(x_ref, wg_ref, wu_ref, wd_ref, g_ref, b_ref, o_ref, xb_ref, *, alpha, nf, col_chunk):
    j = pl.program_id(1)

    @pl.when(j == 0)
    def _():
        xb_ref[...] = x_ref[...].astype(BF16)
        o_ref[...] = jnp.zeros_like(o_ref)

    xb = xb_ref[...]
    gate = _dot(xb, wg_ref[...].astype(BF16))
    up = _dot(xb, wu_ref[...].astype(BF16))
    h = (gate * jax.nn.sigmoid(gate) * up).astype(BF16)
    d = o_ref.shape[1]
    for c in range(d // col_chunk):
        cols = slice(c * col_chunk, (c + 1) * col_chunk)
        o_ref[:, cols] += _dot(h, wd_ref[:, cols].astype(BF16))

    @pl.when(j == nf - 1)
    def _():
        _ln_epilogue(o_ref, x_ref, g_ref, b_ref, alpha, 0.5)


def _ffn_ln(x, w_gate, w_up, w_down, ln_g4, ln_b4, li, si, ln_idx, alpha):
    m, d = x.shape
    f = w_gate.shape[-1]
    tm = _row_tile(m)
    tf = _pick(f, (256, 128)) if tm >= 512 else _pick(f, (512, 256, 128))
    nf = f // tf
    body = functools.partial(_ffn_ln_body, alpha=alpha, nf=nf, col_chunk=_pick(d, (512, 256, 128)))
    return pl.pallas_call(
        body,
        grid=(m // tm, nf),
        in_specs=[
            pl.BlockSpec((tm, d), lambda i, j: (i, 0)),
            pl.BlockSpec((None, None, d, tf), lambda i, j: (li, si, 0, j)),
            pl.BlockSpec((None, None, d, tf), lambda i, j: (li, si, 0, j)),
            pl.BlockSpec((None, None, tf, d), lambda i, j: (li, si, j, 0)),
            pl.BlockSpec((None, None, 1, d), lambda i, j: (li, ln_idx, 0, 0)),
            pl.BlockSpec((None, None, 1, d), lambda i, j: (li, ln_idx, 0, 0)),
        ],
        out_specs=pl.BlockSpec((tm, d), lambda i, j: (i, 0)),
        out_shape=jax.ShapeDtypeStruct((m, d), F32),
        scratch_shapes=[pltpu.VMEM((tm, d), BF16)],
        compiler_params=_params(("parallel", "arbitrary")),
        name="ffn_ln",
    )(x, w_gate, w_up, w_down, ln_g4, ln_b4)


def _glu_out_ln_body(y_ref, res_ref, wglu_ref, bglu_ref, wout_ref, g_ref, b_ref, o_ref, z_ref, *, alpha, nt, tn):
    j = pl.program_id(1)

    @pl.when(j < nt)
    def _():
        cols = pl.ds(pl.multiple_of(j * tn, tn), tn)
        t = _dot(y_ref[...], wglu_ref[...].astype(BF16)) + bglu_ref[...]
        z_ref[:, cols] = (y_ref[:, cols].astype(F32) * jax.nn.sigmoid(t)).astype(BF16)

    @pl.when(j >= nt)
    def _():
        cols = pl.ds(pl.multiple_of((j - nt) * tn, tn), tn)
        o_ref[:, cols] = _dot(z_ref[...], wout_ref[...].astype(BF16))

    @pl.when(j == 2 * nt - 1)
    def _():
        _ln_epilogue(o_ref, res_ref, g_ref, b_ref, alpha, 1.0)


def _glu_out_ln(y, res, glu_w, glu_b3, w_out, ln_g4, ln_b4, sj, li, ln_idx, alpha):
    m, d = res.shape
    tm = _row_tile(m)
    tn = _pick(d, (512, 256, 128))
    nt = d // tn
    body = functools.partial(_glu_out_ln_body, alpha=alpha, nt=nt, tn=tn)
    return pl.pallas_call(
        body,
        grid=(m // tm, 2 * nt),
        in_specs=[
            pl.BlockSpec((tm, d), lambda i, j: (i, 0)),
            pl.BlockSpec((tm, d), lambda i, j: (i, 0)),
            pl.BlockSpec((None, d, tn), lambda i, j: (sj, 0, jnp.minimum(j, nt - 1))),
            pl.BlockSpec((None, 1, tn), lambda i, j: (sj, 0, jnp.minimum(j, nt - 1))),
            pl.BlockSpec((None, d, tn), lambda i, j: (sj, 0, jnp.maximum(j - nt, 0))),
            pl.BlockSpec((None, None, 1, d), lambda i, j: (li, ln_idx, 0, 0)),
            pl.BlockSpec((None, None, 1, d), lambda i, j: (li, ln_idx, 0, 0)),
        ],
        out_specs=pl.BlockSpec((tm, d), lambda i, j: (i, 0)),
        out_shape=jax.ShapeDtypeStruct((m, d), F32),
        scratch_shapes=[pltpu.VMEM((tm, d), BF16)],
        compiler_params=_params(("parallel", "arbitrary")),
        name="s5_glu_out_ln",
    )(y, res, glu_w, glu_b3, w_out, ln_g4, ln_b4)


def _qkv_body(x_ref, w_ref, q_ref, kf_ref, kb_ref, vf_ref, vb_ref, xb_ref, *, scale, nq, nk):
    j = pl.program_id(1)

    @pl.when(j == 0)
    def _():
        xb_ref[...] = x_ref[...].astype(BF16)

    r = _dot(xb_ref[...], w_ref[...].astype(BF16))

    @pl.when(j < nq)
    def _():
        q_ref[...] = (r * scale).astype(q_ref.dtype)

    @pl.when(jnp.logical_and(j >= nq, j < nq + nk))
    def _():
        kf_ref[...] = r
        kb_ref[...] = r.astype(kb_ref.dtype)

    @pl.when(j >= nq + nk)
    def _():
        vf_ref[...] = r
        vb_ref[...] = r.astype(vb_ref.dtype)


def _qkv_proj(x, w3, wj, n_q, n_k, n_v, scale):
    m, d = x.shape
    tm = _row_tile(m)
    tn = _pick(math.gcd(n_q, math.gcd(n_k, n_v)), (512, 256, 128))
    nq, nk, nv = n_q // tn, n_k // tn, n_v // tn
    body = functools.partial(_qkv_body, scale=scale, nq=nq, nk=nk)
    q_map = lambda i, j: (i, jnp.minimum(j, nq - 1))
    k_map = lambda i, j: (i, jnp.clip(j - nq, 0, nk - 1))
    v_map = lambda i, j: (i, jnp.maximum(j - nq - nk, 0))
    return pl.pallas_call(
        body,
        grid=(m // tm, nq + nk + nv),
        in_specs=[
            pl.BlockSpec((tm, d), lambda i, j: (i, 0)),
            pl.BlockSpec((None, d, tn), lambda i, j: (wj, 0, j)),
        ],
        out_specs=[
            pl.BlockSpec((tm, tn), q_map),
            pl.BlockSpec((tm, tn), k_map),
            pl.BlockSpec((tm, tn), k_map),
            pl.BlockSpec((tm, tn), v_map),
            pl.BlockSpec((tm, tn), v_map),
        ],
        out_shape=[
            jax.ShapeDtypeStruct((m, n_q), BF16),
            jax.ShapeDtypeStruct((m, n_k), F32),
            jax.ShapeDtypeStruct((m, n_k), BF16),
            jax.ShapeDtypeStruct((m, n_v), F32),
            jax.ShapeDtypeStruct((m, n_v), BF16),
        ],
        scratch_shapes=[pltpu.VMEM((tm, d), BF16)],
        compiler_params=_params(("parallel", "arbitrary")),
        name="qkv_proj",
    )(x, w3)


def _proj_ln_body(x_ref, res_ref, w_ref, g_ref, b_ref, o_ref, *, alpha, nn, tn):
    j = pl.program_id(1)
    o_ref[:, pl.ds(pl.multiple_of(j * tn, tn), tn)] = _dot(x_ref[...].astype(BF16), w_ref[...].astype(BF16))

    @pl.when(j == nn - 1)
    def _():
        _ln_epilogue(o_ref, res_ref, g_ref, b_ref, alpha, 1.0)


def _proj_ln(x, res, w3, wj, ln_g4, ln_b4, li, ln_idx, alpha):
    m, k = x.shape
    d = res.shape[1]
    tm = _row_tile(m)
    tn = _pick(d, (512, 256, 128))
    nn = d // tn
    body = functools.partial(_proj_ln_body, alpha=alpha, nn=nn, tn=tn)
    return pl.pallas_call(
        body,
        grid=(m // tm, nn),
        in_specs=[
            pl.BlockSpec((tm, k), lambda i, j: (i, 0)),
            pl.BlockSpec((tm, d), lambda i, j: (i, 0)),
            pl.BlockSpec((None, k, tn), lambda i, j: (wj, 0, j)),
            pl.BlockSpec((None, None, 1, d), lambda i, j: (li, ln_idx, 0, 0)),
            pl.BlockSpec((None, None, 1, d), lambda i, j: (li, ln_idx, 0, 0)),
        ],
        out_specs=pl.BlockSpec((tm, d), lambda i, j: (i, 0)),
        out_shape=jax.ShapeDtypeStruct((m, d), F32),
        compiler_params=_params(("parallel", "arbitrary")),
        name="proj_ln",
    )(x, res, w3, ln_g4, ln_b4)


def _pe_body(h_ref, p_ref, wg_ref, wp_ref, o_ref, hb_ref, *, tn):
    j = pl.program_id(1)

    @pl.when(j == 0)
    def _():
        hb_ref[...] = h_ref[...].astype(BF16)

    gate = _dot(hb_ref[...], wg_ref[...].astype(BF16))
    pe = _dot(p_ref[...].astype(BF16), wp_ref[...].astype(BF16))
    hcols = h_ref[:, pl.ds(pl.multiple_of(j * tn, tn), tn)]
    o_ref[...] = hcols + jax.nn.sigmoid(gate) * pe


def _pe_gate(h, p3, w_pe, w_pe_gate, li):
    m, d = h.shape
    pd = p3.shape[-1]
    tm = _row_tile(m)
    tn = _pick(d, (512, 256, 128))
    body = functools.partial(_pe_body, tn=tn)
    return pl.pallas_call(
        body,
        grid=(m // tm, d // tn),
        in_specs=[
            pl.BlockSpec((tm, d), lambda i, j: (i, 0)),
            pl.BlockSpec((None, tm, pd), lambda i, j: (li, i, 0)),
            pl.BlockSpec((None, d, tn), lambda i, j: (li, 0, j)),
            pl.BlockSpec((None, pd, tn), lambda i, j: (li, 0, j)),
        ],
        out_specs=pl.BlockSpec((tm, tn), lambda i, j: (i, j)),
        out_shape=jax.ShapeDtypeStruct((m, d), F32),
        scratch_shapes=[pltpu.VMEM((tm, d), BF16)],
        compiler_params=_params(("parallel", "arbitrary")),
        name="pe_gate",
    )(h, p3, w_pe_gate, w_pe)


def _cmul_add(sr, si, ar, ai, pr, pi):
    return sr + ar * pr - ai * pi, si + ar * pi + ai * pr


def _sublane_scan(sr, si, coef_ref, first, nlog, re, im):
    for k in range(nlog):
        ar, ai = coef_ref[first + k, :, re], coef_ref[first + k, :, im]
        pr = pltpu.roll(sr, 1 << k, axis=0)
        pi = pltpu.roll(si, 1 << k, axis=0)
        sr, si = _cmul_add(sr, si, ar, ai, pr, pi)
    return sr, si


def _s5_in_proj(x_ref, bblk_ref, bu_ref, rows, row_block):
    for rb in range(rows // row_block):
        r = slice(rb * row_block, (rb + 1) * row_block)
        bu_ref[r, :] = _dot(x_ref[r, :].astype(BF16), bblk_ref[...])


def _s5_out_proj(x_ref, s_ref, cblk_ref, d_ref, y_ref, rows, row_block):
    dvec = d_ref[...]
    for rb in range(rows // row_block):
        r = slice(rb * row_block, (rb + 1) * row_block)
        y = _dot(s_ref[r, :].astype(BF16), cblk_ref[...]) + dvec * x_ref[r, :]
        y_ref[r, :] = jax.nn.gelu(y).astype(y_ref.dtype)


def _s5_short_body(x_ref, bblk_ref, cblk_ref, coef_ref, d_ref, s0_ref, y_ref, st_ref, bu_ref, *, rows, nlog,
                   row_block):
    ns = bu_ref.shape[1] // 2
    re, im = slice(0, ns), slice(ns, 2 * ns)
    _s5_in_proj(x_ref, bblk_ref, bu_ref, rows, row_block)
    ar, ai = coef_ref[nlog, 0:1, re], coef_ref[nlog, 0:1, im]
    br, bi = _cmul_add(bu_ref[:, re], bu_ref[:, im], ar, ai, s0_ref[:, re], s0_ref[:, im])
    bu_ref[:, re] = br
    bu_ref[:, im] = bi

    def group(gi, c):
        rws = pl.ds(pl.multiple_of(gi * SUBLANES, SUBLANES), SUBLANES)
        sr, si = _sublane_scan(bu_ref[rws, re], bu_ref[rws, im], coef_ref, 0, nlog, re, im)
        bu_ref[rws, re] = sr
        bu_ref[rws, im] = si
        return c

    lax.fori_loop(0, rows // SUBLANES, group, 0)
    st_ref[...] = bu_ref[...]
    _s5_out_proj(x_ref, bu_ref, cblk_ref, d_ref, y_ref, rows, row_block)


def _s5_long_body(x_ref, bblk_ref, cblk_ref, coef_ref, d_ref, y_ref, st_ref, bu_ref, pw_ref, *, rows, row_block):
    ns = bu_ref.shape[1] // 2
    re, im = slice(0, ns), slice(ns, 2 * ns)
    seg_len = rows // SUBLANES
    _s5_in_proj(x_ref, bblk_ref, bu_ref, rows, row_block)

    ar, ai = coef_ref[0, :, re], coef_ref[0, :, im]

    def step(i, c):
        rws = pl.ds(pl.multiple_of(i * SUBLANES, SUBLANES), SUBLANES)
        sr, si = _cmul_add(bu_ref[rws, re], bu_ref[rws, im], ar, ai, c[0], c[1])
        bu_ref[rws, re] = sr
        bu_ref[rws, im] = si
        return sr, si

    zero = jnp.zeros((SUBLANES, ns), F32)
    er, ei = lax.fori_loop(0, seg_len, step, (zero, zero), unroll=4)

    er, ei = _sublane_scan(er, ei, coef_ref, 1, 3, re, im)
    st_ref[:, re] = er[SUBLANES - 1:SUBLANES, :]
    st_ref[:, im] = ei[SUBLANES - 1:SUBLANES, :]
    first = lax.broadcasted_iota(jnp.int32, (SUBLANES, ns), 0) == 0
    br = jnp.where(first, 0.0, pltpu.roll(er, 1, axis=0))
    bi = jnp.where(first, 0.0, pltpu.roll(ei, 1, axis=0))

    pw_ref[0:SUBLANES, :] = coef_ref[4]
    n = SUBLANES
    while n < seg_len:
        lr, li = pw_ref[n - 1:n, re], pw_ref[n - 1:n, im]
        qr, qi = pw_ref[0:n, re], pw_ref[0:n, im]
        pw_ref[n:2 * n, re] = qr * lr - qi * li
        pw_ref[n:2 * n, im] = qr * li + qi * lr
        n *= 2

    def fix(i, c):
        rws = pl.ds(pl.multiple_of(i * SUBLANES, SUBLANES), SUBLANES)
        sr, si = _cmul_add(bu_ref[rws, re], bu_ref[rws, im], pw_ref[pl.ds(i, 1), re], pw_ref[pl.ds(i, 1), im],
                           br, bi)
        bu_ref[rws, re] = sr
        bu_ref[rws, im] = si
        return c

    lax.fori_loop(0, seg_len, fix, 0, unroll=4)
    _s5_out_proj(x_ref, bu_ref, cblk_ref, d_ref, y_ref, rows, row_block)


def _s5_tables(lam_re, lam_im, log_dt, b_re, b_im, c_re, c_im, seg=None, long_len=None):
    g, p = lam_re.shape
    w = b_re.shape[-1]
    gpc = LANES // w
    nc = g // gpc
    dt = jnp.exp(log_dt)[:, None]
    mag = jnp.exp(lam_re * dt)
    ang = lam_im * dt
    a_re, a_im = mag * jnp.cos(ang), mag * jnp.sin(ang)
    den = lam_re * lam_re + lam_im * lam_im
    coef_re = ((a_re - 1.0) * lam_re + a_im * lam_im) / den
    coef_im = (a_im * lam_re - (a_re - 1.0) * lam_im) / den
    bb_re = coef_re[..., None] * b_re - coef_im[..., None] * b_im
    bb_im = coef_re[..., None] * b_im + coef_im[..., None] * b_re
    eye = jnp.eye(gpc, dtype=F32)

    def blockdiag_in(bb):
        t = bb.reshape(nc, gpc, p, w)
        t = jnp.einsum('cgpw,gh->cgwhp', t, eye)
        return t.reshape(nc, gpc * w, gpc * p)

    def blockdiag_out(cc):
        t = cc.reshape(nc, gpc, w, p)
        t = jnp.einsum('cgwp,gh->cgphw', t, eye)
        return t.reshape(nc, gpc * p, gpc * w)

    bblk = jnp.concatenate([blockdiag_in(bb_re), blockdiag_in(bb_im)], axis=-1).astype(BF16)
    cblk = jnp.concatenate([blockdiag_out(c_re), -blockdiag_out(c_im)], axis=1).astype(BF16)

    def rows_of(zr, zi):
        return jnp.concatenate([zr.reshape(nc, gpc * p), zi.reshape(nc, gpc * p)], axis=-1)

    def csq(zr, zi):
        return zr * zr - zi * zi, 2.0 * zr * zi

    def on_rows(zr, zi):
        z = rows_of(zr, zi)
        return jnp.broadcast_to(z[:, None, :], z.shape[:1] + (SUBLANES,) + z.shape[1:])

    def scan_rows(zr, zi, seg, nlog):
        row = jnp.arange(SUBLANES)
        out = []
        for k in range(nlog):
            keep = ((row % seg) >= (1 << k)).astype(F32)[None, :, None]
            out.append(on_rows(zr, zi) * keep)
            zr, zi = csq(zr, zi)
        return out

    if long_len is None:
        nlog = int(math.log2(seg))
        tabs = scan_rows(a_re, a_im, seg, nlog) + [on_rows(a_re, a_im)]
    else:
        nlog = 3
        sr, si = a_re, a_im
        for _ in range(int(math.log2(long_len // SUBLANES))):
            sr, si = csq(sr, si)
        pws = []
        qr, qi = a_re, a_im
        for _ in range(SUBLANES):
            pws.append(rows_of(qr, qi))
            qr, qi = qr * a_re - qi * a_im, qr * a_im + qi * a_re
        tabs = [on_rows(a_re, a_im)] + scan_rows(sr, si, SUBLANES, nlog) + [jnp.stack(pws, axis=1)]
    coef = jnp.stack(tabs, axis=1)
    return bblk, cblk, coef, nlog


def _s5_core(h3, tables, d_skip2, sj, s0_rows=None):
    bblk, cblk, coef, nlog = tables
    nb, rows, d = h3.shape
    nc = d // LANES
    ns2 = bblk.shape[-1]
    ntab = coef.shape[1]
    has_s0 = s0_rows is not None
    row_block = _pick(rows, (256, 128, 64, 32, 16, 8))
    scratch = [pltpu.VMEM((rows, ns2), F32)]
    if has_s0:
        body = functools.partial(_s5_short_body, rows=rows, nlog=nlog, row_block=row_block)
    else:
        body = functools.partial(_s5_long_body, rows=rows, row_block=row_block)
        scratch.append(pltpu.VMEM((rows // SUBLANES, ns2), F32))
    in_specs = [
        pl.BlockSpec((None, rows, LANES), lambda b, c: (b, 0, c)),
        pl.BlockSpec((None, LANES, ns2), lambda b, c: (c, 0, 0)),
        pl.BlockSpec((None, ns2, LANES), lambda b, c: (c, 0, 0)),
        pl.BlockSpec((None, ntab, SUBLANES, ns2), lambda b, c: (c, 0, 0, 0)),
        pl.BlockSpec((None, 1, LANES), lambda b, c: (sj, 0, c)),
    ]
    args = [h3, bblk, cblk, coef, d_skip2]
    st_rows = rows if has_s0 else 1
    if has_s0:
        in_specs.append(pl.BlockSpec((None, None, rows, ns2), lambda b, c: (b, c, 0, 0)))
        args.append(s0_rows)
    return pl.pallas_call(
        body,
        grid=(nb, nc),
        in_specs=in_specs,
        out_specs=[
            pl.BlockSpec((None, rows, LANES), lambda b, c: (b, 0, c)),
            pl.BlockSpec((None, None, st_rows, ns2), lambda b, c: (b, c, 0, 0)),
        ],
        out_shape=[
            jax.ShapeDtypeStruct((nb, rows, d), BF16),
            jax.ShapeDtypeStruct((nb, nc, st_rows, ns2), F32),
        ],
        scratch_shapes=scratch,
        compiler_params=_params(("parallel", "parallel"), VMEM_LIMIT_SMALL),
        name="s5_core",
    )(*args)


def _subln(o, g, out_scale):
    return o * lax.rsqrt(jnp.mean(o * o, axis=-1, keepdims=True) + LN_EPS) * g * out_scale


def _online_softmax_step(s, c, m, l, acc_ref, v):
    m_new = jnp.maximum(m, jnp.max(s, axis=-1, keepdims=True) + c)
    corr = jnp.exp(m - m_new)
    p = jnp.exp(s - (m_new - c))
    l_new = l * corr + jnp.sum(p, axis=-1, keepdims=True)
    acc_ref[...] = acc_ref[...] * corr + _dot(p.astype(BF16), v)
    return m_new, l_new


def _attn_prompt_body(slopes_ref, lam_ref, q_ref, k_ref, v_ref, g_ref, o_ref, bias_ref, s_ref, *, tq, nq, half,
                      out_scale):
    h = pl.program_id(1)
    neg_slope = -LOG2_E * slopes_ref[h]
    lam = lam_ref[0]
    rel = (lax.broadcasted_iota(jnp.int32, (tq, tq), 0) - lax.broadcasted_iota(jnp.int32, (tq, tq), 1)).astype(F32)
    bias_ref[0] = rel * neg_slope
    bias_ref[1] = jnp.where(rel >= 0, rel * neg_slope, NEG_INF)
    lane = lax.broadcasted_iota(jnp.int32, (tq, q_ref.shape[1]), 1)

    slot = 0
    for qi in range(nq):
        q = q_ref[qi * tq:(qi + 1) * tq, :]
        q1 = jnp.where(lane < half, q, jnp.zeros_like(q))
        q2 = jnp.where(lane >= half, q, jnp.zeros_like(q))
        offs = [neg_slope * float((qi - kj) * tq) for kj in range(qi + 1)]
        m1 = m2 = None
        for kj in range(qi + 1):
            kb = k_ref[kj * tq:(kj + 1) * tq, :]
            bias = bias_ref[1 if kj == qi else 0]
            s1 = _dot_nt(q1, kb) + bias
            s2 = _dot_nt(q2, kb) + bias
            s_ref[0, slot + kj] = s1
            s_ref[1, slot + kj] = s2
            x1 = jnp.max(s1, axis=-1, keepdims=True) + offs[kj]
            x2 = jnp.max(s2, axis=-1, keepdims=True) + offs[kj]
            m1 = x1 if m1 is None else jnp.maximum(m1, x1)
            m2 = x2 if m2 is None else jnp.maximum(m2, x2)
        acc1 = acc2 = l1 = l2 = None
        for kj in range(qi + 1):
            vb = v_ref[kj * tq:(kj + 1) * tq, :]
            p1 = jnp.exp2(s_ref[0, slot + kj] - (m1 - offs[kj]))
            p2 = jnp.exp2(s_ref[1, slot + kj] - (m2 - offs[kj]))
            d1 = _dot(p1.astype(BF16), vb)
            d2 = _dot(p2.astype(BF16), vb)
            r1 = jnp.sum(p1, axis=-1, keepdims=True)
            r2 = jnp.sum(p2, axis=-1, keepdims=True)
            acc1, acc2 = (d1, d2) if acc1 is None else (acc1 + d1, acc2 + d2)
            l1, l2 = (r1, r2) if l1 is None else (l1 + r1, l2 + r2)
        slot += qi + 1
        o = acc1 / l1 - lam * (acc2 / l2)
        o_ref[qi * tq:(qi + 1) * tq, :] = _subln(o, g_ref[...], out_scale).astype(o_ref.dtype)


def _attn_prompt(q, k, v, slopes, lam, subln_g3, aj, nb, seq, n_heads, out_scale):
    m, hd = q.shape
    dh = hd // n_heads
    tq = _pick(seq, (512, 256, 128, 64, 32, 16, 8))
    nq = seq // tq
    body = functools.partial(_attn_prompt_body, tq=tq, nq=nq, half=dh // 2, out_scale=out_scale)
    return pl.pallas_call(
        body,
        grid=(nb, n_heads),
        in_specs=[
            pl.BlockSpec(memory_space=pltpu.SMEM),
            pl.BlockSpec(memory_space=pltpu.SMEM),
            pl.BlockSpec((seq, dh), lambda b, h: (b, h)),
            pl.BlockSpec((seq, dh), lambda b, h: (b, h)),
            pl.BlockSpec((seq, dh), lambda b, h: (b, h)),
            pl.BlockSpec((None, 1, dh), lambda b, h: (aj, 0, 0)),
        ],
        out_specs=pl.BlockSpec((seq, dh), lambda b, h: (b, h)),
        out_shape=jax.ShapeDtypeStruct((m, hd), BF16),
        scratch_shapes=[pltpu.VMEM((2, tq, tq), F32), pltpu.VMEM((2, nq * (nq + 1) // 2, tq, tq), F32)],
        compiler_params=_params(("parallel", "parallel"), VMEM_LIMIT_SMALL),
        name="diff_attn_prompt",
    )(slopes, lam, q, k, v, subln_g3)


def _attn_decode_body(pt_ref, lam_ref, qc_ref, *refs, n_pages, page, gp, n_heads, t_new, out_scale):
    k_refs, v_refs = refs[:gp], refs[gp:2 * gp]
    (kn_ref, vn_ref, cmat_ref, cnew_ref, slope_ref, step_ref, g_ref, o_ref, acc_ref, m_ref, l_ref,
     s_ref) = refs[2 * gp:]
    p = pl.program_id(1)
    past = n_pages * page

    @pl.when(p == 0)
    def _():
        m_ref[...] = jnp.full_like(m_ref, NEG_INF)
        l_ref[...] = jnp.zeros_like(l_ref)
        acc_ref[...] = jnp.zeros_like(acc_ref)

    qc = qc_ref[...]
    slope = slope_ref[...]
    qpos = step_ref[...] + float(past)

    m_old = m_ref[...]
    m_new = m_old
    offs = []
    for g in range(gp):
        s = _dot_nt(qc, k_refs[g][...].astype(BF16)) + cmat_ref[...]
        s_ref[g] = s
        c = slope * (((p * gp + g) * page).astype(F32) - qpos)
        offs.append(c)
        m_new = jnp.maximum(m_new, jnp.max(s, axis=-1, keepdims=True) + c)
    lsum = jnp.zeros_like(m_new)
    pv = jnp.zeros(acc_ref.shape, F32)
    for g in range(gp):
        pe = jnp.exp(s_ref[g] - (m_new - offs[g]))
        lsum = lsum + jnp.sum(pe, axis=-1, keepdims=True)
        pv = pv + _dot(pe.astype(BF16), v_refs[g][...].astype(BF16))
    corr = jnp.exp(m_old - m_new)
    m_ref[...] = m_new
    l_ref[...] = l_ref[...] * corr + lsum
    acc_ref[...] = acc_ref[...] * corr + pv

    @pl.when(p == n_pages // gp - 1)
    def _():
        s = _dot_nt(qc, kn_ref[...]) + cnew_ref[...]
        m_fin, l_fin = _online_softmax_step(s, 0.0, m_ref[...], l_ref[...], acc_ref, vn_ref[...])
        lam = lam_ref[0]
        inv = 1.0 / l_fin
        rph = acc_ref.shape[0] // n_heads
        dh = acc_ref.shape[1]
        for h in range(n_heads):
            blk = acc_ref[h * rph:(h + 1) * rph, :] * inv[h * rph:(h + 1) * rph, :]
            o = blk[0:t_new, :] - lam * blk[rph // 2:rph // 2 + t_new, :]
            o_ref[:, h * dh:(h + 1) * dh] = _subln(o, g_ref[...], out_scale)


def _attn_decode(q, k_new, v_new, cache_k4, cache_v4, page_table, slopes, lam, subln_g3, aj, nb, t_new, n_heads,
                 out_scale):
    hd = q.shape[1]
    dh = hd // n_heads
    half = dh // 2
    prows = cache_k4.shape[2]
    page = prows // n_heads
    n_pages = page_table.shape[1]
    gp = _pick(n_pages, (8, 4, 2, 1))
    rph = SUBLANES
    tslots = SUBLANES
    assert t_new <= rph // 2 and t_new <= tslots
    nr = n_heads * rph
    q4 = jnp.transpose(q.reshape(nb, t_new, n_heads, dh), (0, 2, 1, 3))
    q4 = jnp.pad(q4, ((0, 0), (0, 0), (0, rph // 2 - t_new), (0, 0)))
    lane = jnp.arange(dh)
    qc = jnp.stack([jnp.where(lane < half, q4, 0), jnp.where(lane >= half, q4, 0)], axis=2).reshape(nb, nr, dh)
    pad_tok = ((0, 0), (0, tslots - t_new), (0, 0), (0, 0))
    kn = jnp.pad(k_new.reshape(nb, t_new, n_heads, dh), pad_tok).reshape(nb, tslots * n_heads, dh)
    vn = jnp.pad(v_new.reshape(nb, t_new, n_heads, dh), pad_tok).reshape(nb, tslots * n_heads, dh)
    row = jnp.arange(nr)
    row_head, row_step = row // rph, (row % (rph // 2)).astype(F32)
    slope_rows = slopes[row_head][:, None]
    step_rows = row_step[:, None]
    col = jnp.arange(prows)
    cmat = jnp.where((col % n_heads)[None, :] == row_head[:, None],
                     slope_rows * (col // n_heads).astype(F32)[None, :], NEG_INF)
    coln = jnp.arange(tslots * n_heads)
    tokn = (coln // n_heads).astype(F32)[None, :]
    ok = jnp.logical_and((coln % n_heads)[None, :] == row_head[:, None],
                         jnp.logical_and(tokn < t_new, tokn <= step_rows))
    cnew = jnp.where(ok, -slope_rows * (step_rows - tokn), NEG_INF)
    body = functools.partial(_attn_decode_body, n_pages=n_pages, page=page, gp=gp, n_heads=n_heads, t_new=t_new,
                             out_scale=out_scale)

    def page_spec(g):
        return pl.BlockSpec((None, None, prows, dh), lambda b, p, pt: (aj, pt[b * n_pages + p * gp + g], 0, 0))

    grid_spec = pltpu.PrefetchScalarGridSpec(
        num_scalar_prefetch=1,
        grid=(nb, n_pages // gp),
        in_specs=[
            pl.BlockSpec(memory_space=pltpu.SMEM),
            pl.BlockSpec((None, nr, dh), lambda b, p, pt: (b, 0, 0)),
            *[page_spec(g) for g in range(gp)],
            *[page_spec(g) for g in range(gp)],
            pl.BlockSpec((None, tslots * n_heads, dh), lambda b, p, pt: (b, 0, 0)),
            pl.BlockSpec((None, tslots * n_heads, dh), lambda b, p, pt: (b, 0, 0)),
            pl.BlockSpec((nr, prows), lambda b, p, pt: (0, 0)),
            pl.BlockSpec((nr, tslots * n_heads), lambda b, p, pt: (0, 0)),
            pl.BlockSpec((nr, 1), lambda b, p, pt: (0, 0)),
            pl.BlockSpec((nr, 1), lambda b, p, pt: (0, 0)),
            pl.BlockSpec((None, 1, dh), lambda b, p, pt: (aj, 0, 0)),
        ],
        out_specs=pl.BlockSpec((None, t_new, hd), lambda b, p, pt: (b, 0, 0)),
        scratch_shapes=[pltpu.VMEM((nr, dh), F32), pltpu.VMEM((nr, 1), F32), pltpu.VMEM((nr, 1), F32),
                        pltpu.VMEM((gp, nr, prows), F32)],
    )
    return pl.pallas_call(
        body,
        grid_spec=grid_spec,
        out_shape=jax.ShapeDtypeStruct((nb, t_new, hd), F32),
        compiler_params=_params(("parallel", "arbitrary")),
        name="diff_attn_decode",
    )(page_table.reshape(-1), lam, qc, *([cache_k4] * gp), *([cache_v4] * gp), kn, vn, cmat, cnew, slope_rows,
      step_rows, subln_g3)


def kernel(x_prompt, x_sample, cache_k, cache_v, state_s5_re, state_s5_im, page_table, p_prompt, p_sample,
           ln_g, ln_b, w_ffn_gate, w_ffn_up, w_ffn_down, w_pe, w_pe_gate,
           s5_lambda_re, s5_lambda_im, s5_log_dt, s5_b_re, s5_b_im, s5_c_re, s5_c_im, s5_d,
           s5_glu_w, s5_glu_b, s5_w_out,
           attn_w_qkv, attn_lambda_q1, attn_lambda_k1, attn_lambda_q2, attn_lambda_k2, attn_subln_g, attn_w_o):
    depth = ln_g.shape[0]
    alpha = (2.0 * depth) ** 0.25
    nbp, seq, d = x_prompt.shape
    nbs, t_new, _ = x_sample.shape
    n_attn, n_pool, page, n_heads, qk_dim = cache_k.shape
    v_dim = cache_v.shape[-1]
    half = qk_dim // 2
    attn_scale = 1.0 / math.sqrt(half)
    g_cnt, p_cnt = s5_lambda_re.shape[1:]
    gw = s5_b_re.shape[-1]
    gpc = LANES // gw
    nc = d // LANES
    hq = n_heads * qk_dim

    ln_g4 = ln_g.reshape(depth, ln_g.shape[1], 1, d)
    ln_b4 = ln_b.reshape(depth, ln_b.shape[1], 1, d)
    pp3 = p_prompt.reshape(depth, nbp * seq, -1)
    ps3 = p_sample.reshape(depth, nbs * t_new, -1)
    cache_k4 = cache_k.reshape(n_attn, n_pool, page * n_heads, qk_dim)
    cache_v4 = cache_v.reshape(n_attn, n_pool, page * n_heads, v_dim)
    attn_w_qkv = attn_w_qkv.astype(BF16)
    attn_w_o = attn_w_o.astype(BF16)
    s5_glu_w = s5_glu_w.astype(BF16)
    s5_w_out = s5_w_out.astype(BF16)
    w_pe_gate = w_pe_gate.astype(BF16)
    glu_b3 = s5_glu_b.reshape(s5_glu_b.shape[0], 1, d)
    d_skip3 = s5_d.reshape(s5_d.shape[0], 1, d)
    subln_g3 = attn_subln_g.reshape(n_attn, 1, v_dim)
    slopes = jnp.exp2(-8.0 * jnp.arange(1, n_heads + 1, dtype=F32) / n_heads)

    hp = x_prompt.reshape(nbp * seq, d)
    hs = x_sample.reshape(nbs * t_new, d)
    kp, vp, ksm, vsm = [], [], [], []
    srp, sip, srs, sis = [], [], [], []

    def split_state(st):
        lead = st.shape[:-2]
        st = st.reshape(lead + (nc, 2, gpc, p_cnt))
        return (st[..., 0, :, :].reshape(lead + (g_cnt, p_cnt)), st[..., 1, :, :].reshape(lead + (g_cnt, p_cnt)))

    for i in range(depth):
        j = i // 2
        hp = _ffn_ln(hp, w_ffn_gate, w_ffn_up, w_ffn_down, ln_g4, ln_b4, i, 0, 0, alpha)
        hs = _ffn_ln(hs, w_ffn_gate, w_ffn_up, w_ffn_down, ln_g4, ln_b4, i, 0, 0, alpha)
        if i % 2 == 0:
            prm = (s5_lambda_re[j], s5_lambda_im[j], s5_log_dt[j], s5_b_re[j], s5_b_im[j], s5_c_re[j], s5_c_im[j])
            tabs_p = _s5_tables(*prm, long_len=seq)
            seg_len = seq // SUBLANES
            hperm = jnp.transpose(hp.reshape(nbp, SUBLANES, seg_len, d), (0, 2, 1, 3)).reshape(nbp, seq, d)
            yperm, stp = _s5_core(hperm, tabs_p, d_skip3, j)
            yp = jnp.transpose(yperm.reshape(nbp, seg_len, SUBLANES, d), (0, 2, 1, 3)).reshape(nbp * seq, d)
            re, im = split_state(stp[:, :, 0, :])
            srp.append(re)
            sip.append(im)
            hp = _glu_out_ln(yp, hp, s5_glu_w, glu_b3, s5_w_out, ln_g4, ln_b4, j, i, 1, alpha)
            tabs_s = _s5_tables(*prm, seg=t_new)
            s0 = jnp.concatenate([state_s5_re[j].reshape(nbs, nc, gpc * p_cnt),
                                  state_s5_im[j].reshape(nbs, nc, gpc * p_cnt)], axis=-1)
            s0_rows = jnp.zeros((nbs, t_new, nc, s0.shape[-1]), F32).at[:, 0].set(s0)
            s0_rows = jnp.transpose(s0_rows.reshape(nbs * t_new, nc, -1), (1, 0, 2))[None]
            ys, sts = _s5_core(hs.reshape(1, nbs * t_new, d), tabs_s, d_skip3, j, s0_rows)
            st_last = sts[0].reshape(nc, nbs, t_new, -1)[:, :, t_new - 1, :]
            re, im = split_state(jnp.transpose(st_last, (1, 0, 2)))
            srs.append(re)
            sis.append(im)
            hs = _glu_out_ln(ys.reshape(nbs * t_new, d), hs, s5_glu_w, glu_b3, s5_w_out, ln_g4, ln_b4, j, i, 1,
                             alpha)
        else:
            lam_init = 0.8 - 0.6 * math.exp(-0.3 * i)
            lam = (jnp.exp(jnp.sum(attn_lambda_q1[j] * attn_lambda_k1[j]))
                   - jnp.exp(jnp.sum(attn_lambda_q2[j] * attn_lambda_k2[j])) + lam_init).reshape(1).astype(F32)
            out_scale = 1.0 - lam_init
            qp, kf, kb, vf, vb = _qkv_proj(hp, attn_w_qkv, j, hq, hq, n_heads * v_dim, attn_scale * LOG2_E)
            kp.append(kf.reshape(nbp, seq, n_heads, qk_dim))
            vp.append(vf.reshape(nbp, seq, n_heads, v_dim))
            op = _attn_prompt(qp, kb, vb, slopes, lam, subln_g3, j, nbp, seq, n_heads, out_scale)
            hp = _proj_ln(op, hp, attn_w_o, j, ln_g4, ln_b4, i, 1, alpha)
            qs, kf, kb, vf, vb = _qkv_proj(hs, attn_w_qkv, j, hq, hq, n_heads * v_dim, attn_scale)
            ksm.append(kf.reshape(nbs, t_new, n_heads, qk_dim))
            vsm.append(vf.reshape(nbs, t_new, n_heads, v_dim))
            osm = _attn_decode(qs, kb, vb, cache_k4, cache_v4, page_table, slopes, lam, subln_g3, j, nbs, t_new,
                               n_heads, out_scale)
            hs = _proj_ln(osm.reshape(nbs * t_new, n_heads * v_dim), hs, attn_w_o, j, ln_g4, ln_b4, i, 1, alpha)
        hp = _ffn_ln(hp, w_ffn_gate, w_ffn_up, w_ffn_down, ln_g4, ln_b4, i, 1, 2, alpha)
        hs = _ffn_ln(hs, w_ffn_gate, w_ffn_up, w_ffn_down, ln_g4, ln_b4, i, 1, 2, alpha)
        hp = _pe_gate(hp, pp3, w_pe, w_pe_gate, i)
        hs = _pe_gate(hs, ps3, w_pe, w_pe_gate, i)

    return (hp.reshape(nbp, seq, d), hs.reshape(nbs, t_new, d),
            jnp.stack(kp), jnp.stack(vp), jnp.stack(ksm), jnp.stack(vsm),
            jnp.stack(srp), jnp.stack(sip), jnp.stack(srs), jnp.stack(sis))
```

```python
import functools
import math

import jax
import jax.numpy as jnp
from jax import lax
from jax.experimental import pallas as pl
from jax.experimental.pallas import tpu as pltpu

F32 = jnp.float32
BF16 = jnp.bfloat16
LN_EPS = 1e-5
NEG_INF = -1e30
LOG2_E = math.log2(math.e)
LANES = 128
SUBLANES = 8
VMEM_LIMIT_MAX = 60 * 1024 * 1024
VMEM_LIMIT_BIG = 56 * 1024 * 1024
VMEM_LIMIT_SMALL = 40 * 1024 * 1024
EPILOGUE_ROWS = 128


def _pick(n, cands):
    for c in cands:
        if n % c == 0:
            return c
    return n


def _row_tile(m):
    return _pick(m, (1024, 512, 256, 128, 64, 32, 16, 8))


def _params(sem, vmem=VMEM_LIMIT_BIG):
    return pltpu.CompilerParams(dimension_semantics=sem, vmem_limit_bytes=vmem)


def _dot(a, b):
    return jnp.dot(a, b, preferred_element_type=F32)


def _dot_nt(a, b):
    return lax.dot_general(a, b, (((1,), (1,)), ((), ())), preferred_element_type=F32)


def _layer_norm(z, g, b):
    mu = jnp.mean(z, axis=-1, keepdims=True)
    zc = z - mu
    var = jnp.mean(zc * zc, axis=-1, keepdims=True)
    return zc * lax.rsqrt(var + LN_EPS) * g + b


def _ln_epilogue(o_ref, res_ref, g_ref, b_ref, alpha, scale):
    tm = o_ref.shape[0]
    rb = min(EPILOGUE_ROWS, tm)
    g = g_ref[...]
    b = b_ref[...]

    def chunk(r, carry):
        rows = pl.ds(pl.multiple_of(r * rb, rb), rb)
        z = o_ref[rows, :] if scale == 1.0 else scale * o_ref[rows, :]
        if res_ref is not None:
            z = z + alpha * res_ref[rows, :].astype(F32)
        o_ref[rows, :] = _layer_norm(z, g, b)
        return carry

    lax.fori_loop(0, tm // rb, chunk, 0)


def _ffn_ln_body(x_hbm, xs_ref, wg_ref, wu_ref, wd_ref, g_ref, b_ref, o_ref, os_ref, xin_ref, xb_ref, sem, *,
                 alpha, nf, tm, ms, col_chunk):
    i = pl.program_id(0)
    j = pl.program_id(1)
    n_tiles = pl.num_programs(0)
    d = o_ref.shape[1]
    rb = min(EPILOGUE_ROWS, tm)

    def x_copy(tile):
        return pltpu.make_async_copy(x_hbm.at[pl.ds(tile * tm, tm), :], xin_ref, sem)

    @pl.when(j == 0)
    def _():
        @pl.when(i == 0)
        def _():
            x_copy(0).start()
            xs = xs_ref[...]
            xb_ref[tm:tm + ms, :] = xs.astype(BF16)
            os_ref[...] = (2.0 * alpha) * xs

        x_copy(i).wait()

        def chunk(r, carry):
            rows = pl.ds(pl.multiple_of(r * rb, rb), rb)
            xr = xin_ref[rows, :]
            xb_ref[rows, :] = xr.astype(BF16)
            o_ref[rows, :] = (2.0 * alpha) * xr
            return carry

        lax.fori_loop(0, tm // rb, chunk, 0)

    @pl.when(jnp.logical_and(j == 1, i + 1 < n_tiles))
    def _():
        x_copy(i + 1).start()

    def step(nrows):
        xb = xb_ref[0:nrows, :]
        gate = _dot(xb, wg_ref[...].astype(BF16))
        up = _dot(xb, wu_ref[...].astype(BF16))
        h = (gate * jax.nn.sigmoid(gate) * up).astype(BF16)
        for c in range(d // col_chunk):
            cols = slice(c * col_chunk, (c + 1) * col_chunk)
            r = _dot(h, wd_ref[:, cols].astype(BF16))
            o_ref[:, cols] += r[0:tm, :]
            if nrows > tm:
                os_ref[:, cols] += r[tm:nrows, :]

    @pl.when(i == 0)
    def _():
        step(tm + ms)

    @pl.when(i != 0)
    def _():
        step(tm)

    @pl.when(j == nf - 1)
    def _():
        _ln_epilogue(o_ref, None, g_ref, b_ref, alpha, 0.5)

        @pl.when(i == 0)
        def _():
            os_ref[...] = _layer_norm(0.5 * os_ref[...], g_ref[...], b_ref[...])


def _ffn_ln(x, xs, w_gate, w_up, w_down, ln_g4, ln_b4, li, si, ln_idx, alpha):
    m, d = x.shape
    ms = xs.shape[0]
    f = w_gate.shape[-1]
    tm = _row_tile(m)
    tf = _pick(f, (512, 256, 128))
    nf = f // tf
    assert nf >= 2 and ms % 16 == 0
    body = functools.partial(_ffn_ln_body, alpha=alpha, nf=nf, tm=tm, ms=ms, col_chunk=_pick(d, (512, 256, 128)))
    return pl.pallas_call(
        body,
        grid=(m // tm, nf),
        in_specs=[
            pl.BlockSpec(memory_space=pl.ANY),
            pl.BlockSpec((ms, d), lambda i, j: (0, 0)),
            pl.BlockSpec((None, None, d, tf), lambda i, j: (li, si, 0, j)),
            pl.BlockSpec((None, None, d, tf), lambda i, j: (li, si, 0, j)),
            pl.BlockSpec((None, None, tf, d), lambda i, j: (li, si, j, 0)),
            pl.BlockSpec((None, None, 1, d), lambda i, j: (li, ln_idx, 0, 0)),
            pl.BlockSpec((None, None, 1, d), lambda i, j: (li, ln_idx, 0, 0)),
        ],
        out_specs=[pl.BlockSpec((tm, d), lambda i, j: (i, 0)), pl.BlockSpec((ms, d), lambda i, j: (0, 0))],
        out_shape=[jax.ShapeDtypeStruct((m, d), F32), jax.ShapeDtypeStruct((ms, d), F32)],
        scratch_shapes=[pltpu.VMEM((tm, d), F32), pltpu.VMEM((tm + ms, d), BF16), pltpu.SemaphoreType.DMA(())],
        compiler_params=_params(("arbitrary", "arbitrary"), VMEM_LIMIT_MAX),
        name="ffn_ln",
    )(x, xs, w_gate, w_up, w_down, ln_g4, ln_b4)


def _glu_out_ln_body(y_ref, res_ref, wglu_ref, bglu_ref, wout_ref, g_ref, b_ref, o_ref, z_ref, *, alpha, nt, tn):
    j = pl.program_id(1)

    @pl.when(j < nt)
    def _():
        cols = pl.ds(pl.multiple_of(j * tn, tn), tn)
        t = _dot(y_ref[...], wglu_ref[...].astype(BF16)) + bglu_ref[...]
        z_ref[:, cols] = (y_ref[:, cols].astype(F32) * jax.nn.sigmoid(t)).astype(BF16)

    @pl.when(j >= nt)
    def _():
        cols = pl.ds(pl.multiple_of((j - nt) * tn, tn), tn)
        o_ref[:, cols] = _dot(z_ref[...], wout_ref[...].astype(BF16))

    @pl.when(j == 2 * nt - 1)
    def _():
        _ln_epilogue(o_ref, res_ref, g_ref, b_ref, alpha, 1.0)


def _glu_out_ln(y, res, glu_w, glu_b3, w_out, ln_g4, ln_b4, sj, li, ln_idx, alpha):
    m, d = res.shape
    tm = _row_tile(m)
    tn = _pick(d, (512, 256, 128))
    nt = d // tn
    body = functools.partial(_glu_out_ln_body, alpha=alpha, nt=nt, tn=tn)
    return pl.pallas_call(
        body,
        grid=(m // tm, 2 * nt),
        in_specs=[
            pl.BlockSpec((tm, d), lambda i, j: (i, 0)),
            pl.BlockSpec((tm, d), lambda i, j: (i, 0)),
            pl.BlockSpec((None, d, tn), lambda i, j: (sj, 0, jnp.minimum(j, nt - 1))),
            pl.BlockSpec((None, 1, tn), lambda i, j: (sj, 0, jnp.minimum(j, nt - 1))),
            pl.BlockSpec((None, d, tn), lambda i, j: (sj, 0, jnp.maximum(j - nt, 0))),
            pl.BlockSpec((None, None, 1, d), lambda i, j: (li, ln_idx, 0, 0)),
            pl.BlockSpec((None, None, 1, d), lambda i, j: (li, ln_idx, 0, 0)),
        ],
        out_specs=pl.BlockSpec((tm, d), lambda i, j: (i, 0)),
        out_shape=jax.ShapeDtypeStruct((m, d), F32),
        scratch_shapes=[pltpu.VMEM((tm, d), BF16)],
        compiler_params=_params(("parallel", "arbitrary")),
        name="s5_glu_out_ln",
    )(y, res, glu_w, glu_b3, w_out, ln_g4, ln_b4)


def _qkv_body(x_ref, w_ref, q_ref, kf_ref, kb_ref, vf_ref, vb_ref, xb_ref, *, scale, nq, nk):
    j = pl.program_id(1)

    @pl.when(j == 0)
    def _():
        xb_ref[...] = x_ref[...].astype(BF16)

    r = _dot(xb_ref[...], w_ref[...].astype(BF16))

    @pl.when(j < nq)
    def _():
        q_ref[...] = (r * scale).astype(q_ref.dtype)

    @pl.when(jnp.logical_and(j >= nq, j < nq + nk))
    def _():
        kf_ref[...] = r
        kb_ref[...] = r.astype(kb_ref.dtype)

    @pl.when(j >= nq + nk)
    def _():
        vf_ref[...] = r
        vb_ref[...] = r.astype(vb_ref.dtype)


def _qkv_proj(x, w3, wj, n_q, n_k, n_v, scale):
    m, d = x.shape
    tm = _row_tile(m)
    tn = _pick(math.gcd(n_q, math.gcd(n_k, n_v)), (512, 256, 128))
    nq, nk, nv = n_q // tn, n_k // tn, n_v // tn
    body = functools.partial(_qkv_body, scale=scale, nq=nq, nk=nk)
    q_map = lambda i, j: (i, jnp.minimum(j, nq - 1))
    k_map = lambda i, j: (i, jnp.clip(j - nq, 0, nk - 1))
    v_map = lambda i, j: (i, jnp.maximum(j - nq - nk, 0))
    return pl.pallas_call(
        body,
        grid=(m // tm, nq + nk + nv),
        in_specs=[
            pl.BlockSpec((tm, d), lambda i, j: (i, 0)),
            pl.BlockSpec((None, d, tn), lambda i, j: (wj, 0, j)),
        ],
        out_specs=[
            pl.BlockSpec((tm, tn), q_map),
            pl.BlockSpec((tm, tn), k_map),
            pl.BlockSpec((tm, tn), k_map),
            pl.BlockSpec((tm, tn), v_map),
            pl.BlockSpec((tm, tn), v_map),
        ],
        out_shape=[
            jax.ShapeDtypeStruct((m, n_q), BF16),
            jax.ShapeDtypeStruct((m, n_k), F32),
            jax.ShapeDtypeStruct((m, n_k), BF16),
            jax.ShapeDtypeStruct((m, n_v), F32),
            jax.ShapeDtypeStruct((m, n_v), BF16),
        ],
        scratch_shapes=[pltpu.VMEM((tm, d), BF16)],
        compiler_params=_params(("parallel", "arbitrary")),
        name="qkv_proj",
    )(x, w3)


def _proj_ln_body(x_ref, res_ref, w_ref, g_ref, b_ref, o_ref, *, alpha, nn, tn):
    j = pl.program_id(1)
    o_ref[:, pl.ds(pl.multiple_of(j * tn, tn), tn)] = _dot(x_ref[...].astype(BF16), w_ref[...].astype(BF16))

    @pl.when(j == nn - 1)
    def _():
        _ln_epilogue(o_ref, res_ref, g_ref, b_ref, alpha, 1.0)


def _proj_ln(x, res, w3, wj, ln_g4, ln_b4, li, ln_idx, alpha):
    m, k = x.shape
    d = res.shape[1]
    tm = _row_tile(m)
    tn = _pick(d, (512, 256, 128))
    nn = d // tn
    body = functools.partial(_proj_ln_body, alpha=alpha, nn=nn, tn=tn)
    return pl.pallas_call(
        body,
        grid=(m // tm, nn),
        in_specs=[
            pl.BlockSpec((tm, k), lambda i, j: (i, 0)),
            pl.BlockSpec((tm, d), lambda i, j: (i, 0)),
            pl.BlockSpec((None, k, tn), lambda i, j: (wj, 0, j)),
            pl.BlockSpec((None, None, 1, d), lambda i, j: (li, ln_idx, 0, 0)),
            pl.BlockSpec((None, None, 1, d), lambda i, j: (li, ln_idx, 0, 0)),
        ],
        out_specs=pl.BlockSpec((tm, d), lambda i, j: (i, 0)),
        out_shape=jax.ShapeDtypeStruct((m, d), F32),
        compiler_params=_params(("parallel", "arbitrary")),
        name="proj_ln",
    )(x, res, w3, ln_g4, ln_b4)


def _pe_body(h_ref, p_ref, wg_ref, wp_ref, o_ref, hb_ref, *, tn):
    j = pl.program_id(1)

    @pl.when(j == 0)
    def _():
        hb_ref[...] = h_ref[...].astype(BF16)

    gate = _dot(hb_ref[...], wg_ref[...].astype(BF16))
    pe = _dot(p_ref[...].astype(BF16), wp_ref[...].astype(BF16))
    hcols = h_ref[:, pl.ds(pl.multiple_of(j * tn, tn), tn)]
    o_ref[...] = hcols + jax.nn.sigmoid(gate) * pe


def _pe_gate(h, p3, w_pe, w_pe_gate, li):
    m, d = h.shape
    pd = p3.shape[-1]
    tm = _row_tile(m)
    tn = _pick(d, (512, 256, 128))
    body = functools.partial(_pe_body, tn=tn)
    return pl.pallas_call(
        body,
        grid=(m // tm, d // tn),
        in_specs=[
            pl.BlockSpec((tm, d), lambda i, j: (i, 0)),
            pl.BlockSpec((None, tm, pd), lambda i, j: (li, i, 0)),
            pl.BlockSpec((None, d, tn), lambda i, j: (li, 0, j)),
            pl.BlockSpec((None, pd, tn), lambda i, j: (li, 0, j)),
        ],
        out_specs=pl.BlockSpec((tm, tn), lambda i, j: (i, j)),
        out_shape=jax.ShapeDtypeStruct((m, d), F32),
        scratch_shapes=[pltpu.VMEM((tm, d), BF16)],
        compiler_params=_params(("parallel", "arbitrary")),
        name="pe_gate",
    )(h, p3, w_pe_gate, w_pe)


def _cmul_add(sr, si, ar, ai, pr, pi):
    return sr + ar * pr - ai * pi, si + ar * pi + ai * pr


def _sublane_scan(sr, si, coef_ref, first, nlog, re, im):
    for k in range(nlog):
        ar, ai = coef_ref[first + k, :, re], coef_ref[first + k, :, im]
        pr = pltpu.roll(sr, 1 << k, axis=0)
        pi = pltpu.roll(si, 1 << k, axis=0)
        sr, si = _cmul_add(sr, si, ar, ai, pr, pi)
    return sr, si


def _s5_in_proj(x_ref, bblk_ref, bu_ref, rows, row_block):
    for rb in range(rows // row_block):
        r = slice(rb * row_block, (rb + 1) * row_block)
        bu_ref[r, :] = _dot(x_ref[r, :].astype(BF16), bblk_ref[...])


def _s5_out_proj(x_ref, s_ref, cblk_ref, d_ref, y_ref, rows, row_block):
    dvec = d_ref[...]
    for rb in range(rows // row_block):
        r = slice(rb * row_block, (rb + 1) * row_block)
        y = _dot(s_ref[r, :].astype(BF16), cblk_ref[...]) + dvec * x_ref[r, :]
        y_ref[r, :] = jax.nn.gelu(y).astype(y_ref.dtype)


def _s5_short_body(x_ref, bblk_ref, cblk_ref, coef_ref, d_ref, s0_ref, y_ref, st_ref, bu_ref, *, rows, nlog,
                   row_block):
    ns = bu_ref.shape[1] // 2
    re, im = slice(0, ns), slice(ns, 2 * ns)
    _s5_in_proj(x_ref, bblk_ref, bu_ref, rows, row_block)
    ar, ai = coef_ref[nlog, 0:1, re], coef_ref[nlog, 0:1, im]
    br, bi = _cmul_add(bu_ref[:, re], bu_ref[:, im], ar, ai, s0_ref[:, re], s0_ref[:, im])
    bu_ref[:, re] = br
    bu_ref[:, im] = bi

    def group(gi, c):
        rws = pl.ds(pl.multiple_of(gi * SUBLANES, SUBLANES), SUBLANES)
        sr, si = _sublane_scan(bu_ref[rws, re], bu_ref[rws, im], coef_ref, 0, nlog, re, im)
        bu_ref[rws, re] = sr
        bu_ref[rws, im] = si
        return c

    lax.fori_loop(0, rows // SUBLANES, group, 0)
    st_ref[...] = bu_ref[...]
    _s5_out_proj(x_ref, bu_ref, cblk_ref, d_ref, y_ref, rows, row_block)


def _s5_long_body(x_ref, bblk_ref, cblk_ref, coef_ref, d_ref, y_ref, st_ref, bu_ref, pw_ref, *, rows, row_block):
    ns = bu_ref.shape[1] // 2
    re, im = slice(0, ns), slice(ns, 2 * ns)
    seg_len = rows // SUBLANES
    _s5_in_proj(x_ref, bblk_ref, bu_ref, rows, row_block)

    ar, ai = coef_ref[0, :, re], coef_ref[0, :, im]

    def step(i, c):
        rws = pl.ds(pl.multiple_of(i * SUBLANES, SUBLANES), SUBLANES)
        sr, si = _cmul_add(bu_ref[rws, re], bu_ref[rws, im], ar, ai, c[0], c[1])
        bu_ref[rws, re] = sr
        bu_ref[rws, im] = si
        return sr, si

    zero = jnp.zeros((SUBLANES, ns), F32)
    er, ei = lax.fori_loop(0, seg_len, step, (zero, zero), unroll=4)

    er, ei = _sublane_scan(er, ei, coef_ref, 1, 3, re, im)
    st_ref[:, re] = er[SUBLANES - 1:SUBLANES, :]
    st_ref[:, im] = ei[SUBLANES - 1:SUBLANES, :]
    first = lax.broadcasted_iota(jnp.int32, (SUBLANES, ns), 0) == 0
    br = jnp.where(first, 0.0, pltpu.roll(er, 1, axis=0))
    bi = jnp.where(first, 0.0, pltpu.roll(ei, 1, axis=0))

    pw_ref[0:SUBLANES, :] = coef_ref[4]
    n = SUBLANES
    while n < seg_len:
        lr, li = pw_ref[n - 1:n, re], pw_ref[n - 1:n, im]
        qr, qi = pw_ref[0:n, re], pw_ref[0:n, im]
        pw_ref[n:2 * n, re] = qr * lr - qi * li
        pw_ref[n:2 * n, im] = qr * li + qi * lr
        n *= 2

    def fix(i, c):
        rws = pl.ds(pl.multiple_of(i * SUBLANES, SUBLANES), SUBLANES)
        sr, si = _cmul_add(bu_ref[rws, re], bu_ref[rws, im], pw_ref[pl.ds(i, 1), re], pw_ref[pl.ds(i, 1), im],
                           br, bi)
        bu_ref[rws, re] = sr
        bu_ref[rws, im] = si
        return c

    lax.fori_loop(0, seg_len, fix, 0, unroll=4)
    _s5_out_proj(x_ref, bu_ref, cblk_ref, d_ref, y_ref, rows, row_block)


def _s5_tables(lam_re, lam_im, log_dt, b_re, b_im, c_re, c_im, seg=None, long_len=None):
    g, p = lam_re.shape
    w = b_re.shape[-1]
    gpc = LANES // w
    nc = g // gpc
    dt = jnp.exp(log_dt)[:, None]
    mag = jnp.exp(lam_re * dt)
    ang = lam_im * dt
    a_re, a_im = mag * jnp.cos(ang), mag * jnp.sin(ang)
    den = lam_re * lam_re + lam_im * lam_im
    coef_re = ((a_re - 1.0) * lam_re + a_im * lam_im) / den
    coef_im = (a_im * lam_re - (a_re - 1.0) * lam_im) / den
    bb_re = coef_re[..., None] * b_re - coef_im[..., None] * b_im
    bb_im = coef_re[..., None] * b_im + coef_im[..., None] * b_re
    eye = jnp.eye(gpc, dtype=F32)

    def blockdiag_in(bb):
        t = bb.reshape(nc, gpc, p, w)
        t = jnp.einsum('cgpw,gh->cgwhp', t, eye)
        return t.reshape(nc, gpc * w, gpc * p)

    def blockdiag_out(cc):
        t = cc.reshape(nc, gpc, w, p)
        t = jnp.einsum('cgwp,gh->cgphw', t, eye)
        return t.reshape(nc, gpc * p, gpc * w)

    bblk = jnp.concatenate([blockdiag_in(bb_re), blockdiag_in(bb_im)], axis=-1).astype(BF16)
    cblk = jnp.concatenate([blockdiag_out(c_re), -blockdiag_out(c_im)], axis=1).astype(BF16)

    def rows_of(zr, zi):
        return jnp.concatenate([zr.reshape(nc, gpc * p), zi.reshape(nc, gpc * p)], axis=-1)

    def csq(zr, zi):
        return zr * zr - zi * zi, 2.0 * zr * zi

    def on_rows(zr, zi):
        z = rows_of(zr, zi)
        return jnp.broadcast_to(z[:, None, :], z.shape[:1] + (SUBLANES,) + z.shape[1:])

    def scan_rows(zr, zi, seg, nlog):
        row = jnp.arange(SUBLANES)
        out = []
        for k in range(nlog):
            keep = ((row % seg) >= (1 << k)).astype(F32)[None, :, None]
            out.append(on_rows(zr, zi) * keep)
            zr, zi = csq(zr, zi)
        return out

    if long_len is None:
        nlog = int(math.log2(seg))
        tabs = scan_rows(a_re, a_im, seg, nlog) + [on_rows(a_re, a_im)]
    else:
        nlog = 3
        sr, si = a_re, a_im
        for _ in range(int(math.log2(long_len // SUBLANES))):
            sr, si = csq(sr, si)
        pws = []
        qr, qi = a_re, a_im
        for _ in range(SUBLANES):
            pws.append(rows_of(qr, qi))
            qr, qi = qr * a_re - qi * a_im, qr * a_im + qi * a_re
        tabs = [on_rows(a_re, a_im)] + scan_rows(sr, si, SUBLANES, nlog) + [jnp.stack(pws, axis=1)]
    coef = jnp.stack(tabs, axis=1)
    return bblk, cblk, coef, nlog


def _s5_core(h3, tables, d_skip2, sj, s0_rows=None):
    bblk, cblk, coef, nlog = tables
    nb, rows, d = h3.shape
    nc = d // LANES
    ns2 = bblk.shape[-1]
    ntab = coef.shape[1]
    has_s0 = s0_rows is not None
    row_block = _pick(rows, (256, 128, 64, 32, 16, 8))
    scratch = [pltpu.VMEM((rows, ns2), F32)]
    if has_s0:
        body = functools.partial(_s5_short_body, rows=rows, nlog=nlog, row_block=row_block)
    else:
        body = functools.partial(_s5_long_body, rows=rows, row_block=row_block)
        scratch.append(pltpu.VMEM((rows // SUBLANES, ns2), F32))
    in_specs = [
        pl.BlockSpec((None, rows, LANES), lambda b, c: (b, 0, c)),
        pl.BlockSpec((None, LANES, ns2), lambda b, c: (c, 0, 0)),
        pl.BlockSpec((None, ns2, LANES), lambda b, c: (c, 0, 0)),
        pl.BlockSpec((None, ntab, SUBLANES, ns2), lambda b, c: (c, 0, 0, 0)),
        pl.BlockSpec((None, 1, LANES), lambda b, c: (sj, 0, c)),
    ]
    args = [h3, bblk, cblk, coef, d_skip2]
    st_rows = rows if has_s0 else 1
    if has_s0:
        in_specs.append(pl.BlockSpec((None, None, rows, ns2), lambda b, c: (b, c, 0, 0)))
        args.append(s0_rows)
    return pl.pallas_call(
        body,
        grid=(nb, nc),
        in_specs=in_specs,
        out_specs=[
            pl.BlockSpec((None, rows, LANES), lambda b, c: (b, 0, c)),
            pl.BlockSpec((None, None, st_rows, ns2), lambda b, c: (b, c, 0, 0)),
        ],
        out_shape=[
            jax.ShapeDtypeStruct((nb, rows, d), BF16),
            jax.ShapeDtypeStruct((nb, nc, st_rows, ns2), F32),
        ],
        scratch_shapes=scratch,
        compiler_params=_params(("parallel", "parallel"), VMEM_LIMIT_SMALL),
        name="s5_core",
    )(*args)


def _subln(o, g, out_scale):
    return o * lax.rsqrt(jnp.mean(o * o, axis=-1, keepdims=True) + LN_EPS) * g * out_scale


def _online_softmax_step(s, c, m, l, acc_ref, v):
    m_new = jnp.maximum(m, jnp.max(s, axis=-1, keepdims=True) + c)
    corr = jnp.exp(m - m_new)
    p = jnp.exp(s - (m_new - c))
    l_new = l * corr + jnp.sum(p, axis=-1, keepdims=True)
    acc_ref[...] = acc_ref[...] * corr + _dot(p.astype(BF16), v)
    return m_new, l_new


def _attn_prompt_body(slopes_ref, lam_ref, q_ref, k_ref, v_ref, g_ref, o_ref, bias_ref, s_ref, *, tq, nq, half,
                      out_scale):
    h = pl.program_id(1)
    neg_slope = -LOG2_E * slopes_ref[h]
    lam = lam_ref[0]
    rel = (lax.broadcasted_iota(jnp.int32, (tq, tq), 0) - lax.broadcasted_iota(jnp.int32, (tq, tq), 1)).astype(F32)
    bias_ref[0] = rel * neg_slope
    bias_ref[1] = jnp.where(rel >= 0, rel * neg_slope, NEG_INF)
    lane = lax.broadcasted_iota(jnp.int32, (tq, q_ref.shape[1]), 1)

    slot = 0
    for qi in range(nq):
        q = q_ref[qi * tq:(qi + 1) * tq, :]
        q1 = jnp.where(lane < half, q, jnp.zeros_like(q))
        q2 = jnp.where(lane >= half, q, jnp.zeros_like(q))
        offs = [neg_slope * float((qi - kj) * tq) for kj in range(qi + 1)]
        m1 = m2 = None
        for kj in range(qi + 1):
            kb = k_ref[kj * tq:(kj + 1) * tq, :]
            bias = bias_ref[1 if kj == qi else 0]
            s1 = _dot_nt(q1, kb) + bias
            s2 = _dot_nt(q2, kb) + bias
            s_ref[0, slot + kj] = s1
            s_ref[1, slot + kj] = s2
            x1 = jnp.max(s1, axis=-1, keepdims=True) + offs[kj]
            x2 = jnp.max(s2, axis=-1, keepdims=True) + offs[kj]
            m1 = x1 if m1 is None else jnp.maximum(m1, x1)
            m2 = x2 if m2 is None else jnp.maximum(m2, x2)
        acc1 = acc2 = l1 = l2 = None
        for kj in range(qi + 1):
            vb = v_ref[kj * tq:(kj + 1) * tq, :]
            p1 = jnp.exp2(s_ref[0, slot + kj] - (m1 - offs[kj]))
            p2 = jnp.exp2(s_ref[1, slot + kj] - (m2 - offs[kj]))
            d1 = _dot(p1.astype(BF16), vb)
            d2 = _dot(p2.astype(BF16), vb)
            r1 = jnp.sum(p1, axis=-1, keepdims=True)
            r2 = jnp.sum(p2, axis=-1, keepdims=True)
            acc1, acc2 = (d1, d2) if acc1 is None else (acc1 + d1, acc2 + d2)
            l1, l2 = (r1, r2) if l1 is None else (l1 + r1, l2 + r2)
        slot += qi + 1
        o = acc1 / l1 - lam * (acc2 / l2)
        o_ref[qi * tq:(qi + 1) * tq, :] = _subln(o, g_ref[...], out_scale).astype(o_ref.dtype)


def _attn_prompt(q, k, v, slopes, lam, subln_g3, aj, nb, seq, n_heads, out_scale):
    m, hd = q.shape
    dh = hd // n_heads
    tq = _pick(seq, (512, 256, 128, 64, 32, 16, 8))
    nq = seq // tq
    body = functools.partial(_attn_prompt_body, tq=tq, nq=nq, half=dh // 2, out_scale=out_scale)
    return pl.pallas_call(
        body,
        grid=(nb, n_heads),
        in_specs=[
            pl.BlockSpec(memory_space=pltpu.SMEM),
            pl.BlockSpec(memory_space=pltpu.SMEM),
            pl.BlockSpec((seq, dh), lambda b, h: (b, h)),
            pl.BlockSpec((seq, dh), lambda b, h: (b, h)),
            pl.BlockSpec((seq, dh), lambda b, h: (b, h)),
            pl.BlockSpec((None, 1, dh), lambda b, h: (aj, 0, 0)),
        ],
        out_specs=pl.BlockSpec((seq, dh), lambda b, h: (b, h)),
        out_shape=jax.ShapeDtypeStruct((m, hd), BF16),
        scratch_shapes=[pltpu.VMEM((2, tq, tq), F32), pltpu.VMEM((2, nq * (nq + 1) // 2, tq, tq), F32)],
        compiler_params=_params(("parallel", "parallel"), VMEM_LIMIT_SMALL),
        name="diff_attn_prompt",
    )(slopes, lam, q, k, v, subln_g3)


def _attn_decode_body(pt_ref, lam_ref, qc_ref, *refs, n_pages, page, gp, n_heads, t_new, out_scale):
    k_refs, v_refs = refs[:gp], refs[gp:2 * gp]
    (kn_ref, vn_ref, cmat_ref, cnew_ref, slope_ref, step_ref, g_ref, o_ref, acc_ref, m_ref, l_ref,
     s_ref) = refs[2 * gp:]
    p = pl.program_id(1)
    past = n_pages * page

    @pl.when(p == 0)
    def _():
        m_ref[...] = jnp.full_like(m_ref, NEG_INF)
        l_ref[...] = jnp.zeros_like(l_ref)
        acc_ref[...] = jnp.zeros_like(acc_ref)

    qc = qc_ref[...]
    slope = slope_ref[...]
    qpos = step_ref[...] + float(past)

    m_old = m_ref[...]
    m_new = m_old
    offs = []
    for g in range(gp):
        s = _dot_nt(qc, k_refs[g][...].astype(BF16)) + cmat_ref[...]
        s_ref[g] = s
        c = slope * (((p * gp + g) * page).astype(F32) - qpos)
        offs.append(c)
        m_new = jnp.maximum(m_new, jnp.max(s, axis=-1, keepdims=True) + c)
    lsum = jnp.zeros_like(m_new)
    pv = jnp.zeros(acc_ref.shape, F32)
    for g in range(gp):
        pe = jnp.exp(s_ref[g] - (m_new - offs[g]))
        lsum = lsum + jnp.sum(pe, axis=-1, keepdims=True)
        pv = pv + _dot(pe.astype(BF16), v_refs[g][...].astype(BF16))
    corr = jnp.exp(m_old - m_new)
    m_ref[...] = m_new
    l_ref[...] = l_ref[...] * corr + lsum
    acc_ref[...] = acc_ref[...] * corr + pv

    @pl.when(p == n_pages // gp - 1)
    def _():
        s = _dot_nt(qc, kn_ref[...]) + cnew_ref[...]
        m_fin, l_fin = _online_softmax_step(s, 0.0, m_ref[...], l_ref[...], acc_ref, vn_ref[...])
        lam = lam_ref[0]
        inv = 1.0 / l_fin
        rph = acc_ref.shape[0] // n_heads
        dh = acc_ref.shape[1]
        for h in range(n_heads):
            blk = acc_ref[h * rph:(h + 1) * rph, :] * inv[h * rph:(h + 1) * rph, :]
            o = blk[0:t_new, :] - lam * blk[rph // 2:rph // 2 + t_new, :]
            o_ref[:, h * dh:(h + 1) * dh] = _subln(o, g_ref[...], out_scale)


def _attn_decode(q, k_new, v_new, cache_k4, cache_v4, page_table, slopes, lam, subln_g3, aj, nb, t_new, n_heads,
                 out_scale):
    hd = q.shape[1]
    dh = hd // n_heads
    half = dh // 2
    prows = cache_k4.shape[2]
    page = prows // n_heads
    n_pages = page_table.shape[1]
    gp = _pick(n_pages, (8, 4, 2, 1))
    rph = SUBLANES
    tslots = SUBLANES
    assert t_new <= rph // 2 and t_new <= tslots
    nr = n_heads * rph
    q4 = jnp.transpose(q.reshape(nb, t_new, n_heads, dh), (0, 2, 1, 3))
    q4 = jnp.pad(q4, ((0, 0), (0, 0), (0, rph // 2 - t_new), (0, 0)))
    lane = jnp.arange(dh)
    qc = jnp.stack([jnp.where(lane < half, q4, 0), jnp.where(lane >= half, q4, 0)], axis=2).reshape(nb, nr, dh)
    pad_tok = ((0, 0), (0, tslots - t_new), (0, 0), (0, 0))
    kn = jnp.pad(k_new.reshape(nb, t_new, n_heads, dh), pad_tok).reshape(nb, tslots * n_heads, dh)
    vn = jnp.pad(v_new.reshape(nb, t_new, n_heads, dh), pad_tok).reshape(nb, tslots * n_heads, dh)
    row = jnp.arange(nr)
    row_head, row_step = row // rph, (row % (rph // 2)).astype(F32)
    slope_rows = slopes[row_head][:, None]
    step_rows = row_step[:, None]
    col = jnp.arange(prows)
    cmat = jnp.where((col % n_heads)[None, :] == row_head[:, None],
                     slope_rows * (col // n_heads).astype(F32)[None, :], NEG_INF)
    coln = jnp.arange(tslots * n_heads)
    tokn = (coln // n_heads).astype(F32)[None, :]
    ok = jnp.logical_and((coln % n_heads)[None, :] == row_head[:, None],
                         jnp.logical_and(tokn < t_new, tokn <= step_rows))
    cnew = jnp.where(ok, -slope_rows * (step_rows - tokn), NEG_INF)
    body = functools.partial(_attn_decode_body, n_pages=n_pages, page=page, gp=gp, n_heads=n_heads, t_new=t_new,
                             out_scale=out_scale)

    def page_spec(g):
        return pl.BlockSpec((None, None, prows, dh), lambda b, p, pt: (aj, pt[b * n_pages + p * gp + g], 0, 0))

    grid_spec = pltpu.PrefetchScalarGridSpec(
        num_scalar_prefetch=1,
        grid=(nb, n_pages // gp),
        in_specs=[
            pl.BlockSpec(memory_space=pltpu.SMEM),
            pl.BlockSpec((None, nr, dh), lambda b, p, pt: (b, 0, 0)),
            *[page_spec(g) for g in range(gp)],
            *[page_spec(g) for g in range(gp)],
            pl.BlockSpec((None, tslots * n_heads, dh), lambda b, p, pt: (b, 0, 0)),
            pl.BlockSpec((None, tslots * n_heads, dh), lambda b, p, pt: (b, 0, 0)),
            pl.BlockSpec((nr, prows), lambda b, p, pt: (0, 0)),
            pl.BlockSpec((nr, tslots * n_heads), lambda b, p, pt: (0, 0)),
            pl.BlockSpec((nr, 1), lambda b, p, pt: (0, 0)),
            pl.BlockSpec((nr, 1), lambda b, p, pt: (0, 0)),
            pl.BlockSpec((None, 1, dh), lambda b, p, pt: (aj, 0, 0)),
        ],
        out_specs=pl.BlockSpec((None, t_new, hd), lambda b, p, pt: (b, 0, 0)),
        scratch_shapes=[pltpu.VMEM((nr, dh), F32), pltpu.VMEM((nr, 1), F32), pltpu.VMEM((nr, 1), F32),
                        pltpu.VMEM((gp, nr, prows), F32)],
    )
    return pl.pallas_call(
        body,
        grid_spec=grid_spec,
        out_shape=jax.ShapeDtypeStruct((nb, t_new, hd), F32),
        compiler_params=_params(("parallel", "arbitrary")),
        name="diff_attn_decode",
    )(page_table.reshape(-1), lam, qc, *([cache_k4] * gp), *([cache_v4] * gp), kn, vn, cmat, cnew, slope_rows,
      step_rows, subln_g3)


def kernel(x_prompt, x_sample, cache_k, cache_v, state_s5_re, state_s5_im, page_table, p_prompt, p_sample,
           ln_g, ln_b, w_ffn_gate, w_ffn_up, w_ffn_down, w_pe, w_pe_gate,
           s5_lambda_re, s5_lambda_im, s5_log_dt, s5_b_re, s5_b_im, s5_c_re, s5_c_im, s5_d,
           s5_glu_w, s5_glu_b, s5_w_out,
           attn_w_qkv, attn_lambda_q1, attn_lambda_k1, attn_lambda_q2, attn_lambda_k2, attn_subln_g, attn_w_o):
    depth = ln_g.shape[0]
    alpha = (2.0 * depth) ** 0.25
    nbp, seq, d = x_prompt.shape
    nbs, t_new, _ = x_sample.shape
    n_attn, n_pool, page, n_heads, qk_dim = cache_k.shape
    v_dim = cache_v.shape[-1]
    half = qk_dim // 2
    attn_scale = 1.0 / math.sqrt(half)
    g_cnt, p_cnt = s5_lambda_re.shape[1:]
    gw = s5_b_re.shape[-1]
    gpc = LANES // gw
    nc = d // LANES
    hq = n_heads * qk_dim

    ln_g4 = ln_g.reshape(depth, ln_g.shape[1], 1, d)
    ln_b4 = ln_b.reshape(depth, ln_b.shape[1], 1, d)
    pp3 = p_prompt.reshape(depth, nbp * seq, -1)
    ps3 = p_sample.reshape(depth, nbs * t_new, -1)
    cache_k4 = cache_k.reshape(n_attn, n_pool, page * n_heads, qk_dim)
    cache_v4 = cache_v.reshape(n_attn, n_pool, page * n_heads, v_dim)
    attn_w_qkv = attn_w_qkv.astype(BF16)
    attn_w_o = attn_w_o.astype(BF16)
    s5_glu_w = s5_glu_w.astype(BF16)
    s5_w_out = s5_w_out.astype(BF16)
    w_pe_gate = w_pe_gate.astype(BF16)
    glu_b3 = s5_glu_b.reshape(s5_glu_b.shape[0], 1, d)
    d_skip3 = s5_d.reshape(s5_d.shape[0], 1, d)
    subln_g3 = attn_subln_g.reshape(n_attn, 1, v_dim)
    slopes = jnp.exp2(-8.0 * jnp.arange(1, n_heads + 1, dtype=F32) / n_heads)

    hp = x_prompt.reshape(nbp * seq, d)
    hs = x_sample.reshape(nbs * t_new, d)
    kp, vp, ksm, vsm = [], [], [], []
    srp, sip, srs, sis = [], [], [], []

    def split_state(st):
        lead = st.shape[:-2]
        st = st.reshape(lead + (nc, 2, gpc, p_cnt))
        return (st[..., 0, :, :].reshape(lead + (g_cnt, p_cnt)), st[..., 1, :, :].reshape(lead + (g_cnt, p_cnt)))

    for i in range(depth):
        j = i // 2
        hp, hs = _ffn_ln(hp, hs, w_ffn_gate, w_ffn_up, w_ffn_down, ln_g4, ln_b4, i, 0, 0, alpha)
        if i % 2 == 0:
            prm = (s5_lambda_re[j], s5_lambda_im[j], s5_log_dt[j], s5_b_re[j], s5_b_im[j], s5_c_re[j], s5_c_im[j])
            tabs_p = _s5_tables(*prm, long_len=seq)
            seg_len = seq // SUBLANES
            hperm = jnp.transpose(hp.reshape(nbp, SUBLANES, seg_len, d), (0, 2, 1, 3)).reshape(nbp, seq, d)
            yperm, stp = _s5_core(hperm, tabs_p, d_skip3, j)
            yp = jnp.transpose(yperm.reshape(nbp, seg_len, SUBLANES, d), (0, 2, 1, 3)).reshape(nbp * seq, d)
            re, im = split_state(stp[:, :, 0, :])
            srp.append(re)
            sip.append(im)
            hp = _glu_out_ln(yp, hp, s5_glu_w, glu_b3, s5_w_out, ln_g4, ln_b4, j, i, 1, alpha)
            tabs_s = _s5_tables(*prm, seg=t_new)
            s0 = jnp.concatenate([state_s5_re[j].reshape(nbs, nc, gpc * p_cnt),
                                  state_s5_im[j].reshape(nbs, nc, gpc * p_cnt)], axis=-1)
            s0_rows = jnp.zeros((nbs, t_new, nc, s0.shape[-1]), F32).at[:, 0].set(s0)
            s0_rows = jnp.transpose(s0_rows.reshape(nbs * t_new, nc, -1), (1, 0, 2))[None]
            ys, sts = _s5_core(hs.reshape(1, nbs * t_new, d), tabs_s, d_skip3, j, s0_rows)
            st_last = sts[0].reshape(nc, nbs, t_new, -1)[:, :, t_new - 1, :]
            re, im = split_state(jnp.transpose(st_last, (1, 0, 2)))
            srs.append(re)
            sis.append(im)
            hs = _glu_out_ln(ys.reshape(nbs * t_new, d), hs, s5_glu_w, glu_b3, s5_w_out, ln_g4, ln_b4, j, i, 1,
                             alpha)
        else:
            lam_init = 0.8 - 0.6 * math.exp(-0.3 * i)
            lam = (jnp.exp(jnp.sum(attn_lambda_q1[j] * attn_lambda_k1[j]))
                   - jnp.exp(jnp.sum(attn_lambda_q2[j] * attn_lambda_k2[j])) + lam_init).reshape(1).astype(F32)
            out_scale = 1.0 - lam_init
            qp, kf, kb, vf, vb = _qkv_proj(hp, attn_w_qkv, j, hq, hq, n_heads * v_dim, attn_scale * LOG2_E)
            kp.append(kf.reshape(nbp, seq, n_heads, qk_dim))
            vp.append(vf.reshape(nbp, seq, n_heads, v_dim))
            op = _attn_prompt(qp, kb, vb, slopes, lam, subln_g3, j, nbp, seq, n_heads, out_scale)
            hp = _proj_ln(op, hp, attn_w_o, j, ln_g4, ln_b4, i, 1, alpha)
            qs, kf, kb, vf, vb = _qkv_proj(hs, attn_w_qkv, j, hq, hq, n_heads * v_dim, attn_scale)
            ksm.append(kf.reshape(nbs, t_new, n_heads, qk_dim))
            vsm.append(vf.reshape(nbs, t_new, n_heads, v_dim))
            osm = _attn_decode(qs, kb, vb, cache_k4, cache_v4, page_table, slopes, lam, subln_g3, j, nbs, t_new,
                               n_heads, out_scale)
            hs = _proj_ln(osm.reshape(nbs * t_new, n_heads * v_dim), hs, attn_w_o, j, ln_g4, ln_b4, i, 1, alpha)
        hp, hs = _ffn_ln(hp, hs, w_ffn_gate, w_ffn_up, w_ffn_down, ln_g4, ln_b4, i, 1, 2, alpha)
        hp = _pe_gate(hp, pp3, w_pe, w_pe_gate, i)
        hs = _pe_gate(hs, ps3, w_pe, w_pe_gate, i)

    return (hp.reshape(nbp, seq, d), hs.reshape(nbs, t_new, d),
            jnp.stack(kp), jnp.stack(vp), jnp.stack(ksm), jnp.stack(vsm),
            jnp.stack(srp), jnp.stack(sip), jnp.stack(srs), jnp.stack(sis))
```

```python
import functools
import math

import jax
import jax.numpy as jnp
from jax import lax
from jax.experimental import pallas as pl
from jax.experimental.pallas import tpu as pltpu

F32 = jnp.float32
BF16 = jnp.bfloat16
LN_EPS = 1e-5
NEG_INF = -1e30
LOG2_E = math.log2(math.e)
LANES = 128
S5_CHUNK = 2 * LANES
SUBLANES = 8
VMEM_LIMIT_MAX = 60 * 1024 * 1024
VMEM_LIMIT_BIG = 56 * 1024 * 1024
VMEM_LIMIT_SMALL = 40 * 1024 * 1024
EPILOGUE_ROWS = 128
ROW_CHUNK = 128


def _pick(n, cands):
    for c in cands:
        if n % c == 0:
            return c
    return n


def _row_tile(m):
    return _pick(m, (1024, 512, 256, 128, 64, 32, 16, 8))


def _params(sem, vmem=VMEM_LIMIT_BIG):
    return pltpu.CompilerParams(dimension_semantics=sem, vmem_limit_bytes=vmem)


def _dot(a, b):
    return jnp.dot(a, b, preferred_element_type=F32)


def _dot_nt(a, b):
    return lax.dot_general(a, b, (((1,), (1,)), ((), ())), preferred_element_type=F32)


def _layer_norm(z, g, b):
    mu = jnp.mean(z, axis=-1, keepdims=True)
    zc = z - mu
    var = jnp.mean(zc * zc, axis=-1, keepdims=True)
    return zc * lax.rsqrt(var + LN_EPS) * g + b


def _ln_epilogue(o_ref, res_ref, g_ref, b_ref, alpha, scale):
    tm = o_ref.shape[0]
    rb = min(EPILOGUE_ROWS, tm)
    g = g_ref[...]
    b = b_ref[...]

    def chunk(r, carry):
        rows = pl.ds(pl.multiple_of(r * rb, rb), rb)
        z = o_ref[rows, :] if scale == 1.0 else scale * o_ref[rows, :]
        if res_ref is not None:
            z = z + alpha * res_ref[rows, :].astype(F32)
        o_ref[rows, :] = _layer_norm(z, g, b)
        return carry

    lax.fori_loop(0, tm // rb, chunk, 0)


def _ffn_ln_body(x_hbm, xs_ref, wg_ref, wu_ref, wd_ref, g_ref, b_ref, o_ref, os_ref, xin_ref, xb_ref, sem, *,
                 alpha, nf, tm, ms, col_chunk):
    i = pl.program_id(0)
    j = pl.program_id(1)
    n_tiles = pl.num_programs(0)
    d = o_ref.shape[1]
    rb = min(EPILOGUE_ROWS, tm)

    def x_copy(tile):
        return pltpu.make_async_copy(x_hbm.at[pl.ds(tile * tm, tm), :], xin_ref, sem)

    @pl.when(j == 0)
    def _():
        @pl.when(i == 0)
        def _():
            x_copy(0).start()
            xs = xs_ref[...]
            xb_ref[tm:tm + ms, :] = xs.astype(BF16)
            os_ref[...] = (2.0 * alpha) * xs

        x_copy(i).wait()

        def chunk(r, carry):
            rows = pl.ds(pl.multiple_of(r * rb, rb), rb)
            xr = xin_ref[rows, :]
            xb_ref[rows, :] = xr.astype(BF16)
            o_ref[rows, :] = (2.0 * alpha) * xr
            return carry

        lax.fori_loop(0, tm // rb, chunk, 0)

    @pl.when(jnp.logical_and(j == 1, i + 1 < n_tiles))
    def _():
        x_copy(i + 1).start()

    def step(nrows):
        xb = xb_ref[0:nrows, :]
        gate = _dot(xb, wg_ref[...].astype(BF16))
        up = _dot(xb, wu_ref[...].astype(BF16))
        h = (gate * jax.nn.sigmoid(gate) * up).astype(BF16)
        for c in range(d // col_chunk):
            cols = slice(c * col_chunk, (c + 1) * col_chunk)
            r = _dot(h, wd_ref[:, cols].astype(BF16))
            o_ref[:, cols] += r[0:tm, :]
            if nrows > tm:
                os_ref[:, cols] += r[tm:nrows, :]

    @pl.when(i == 0)
    def _():
        step(tm + ms)

    @pl.when(i != 0)
    def _():
        step(tm)

    @pl.when(j == nf - 1)
    def _():
        _ln_epilogue(o_ref, None, g_ref, b_ref, alpha, 0.5)

        @pl.when(i == 0)
        def _():
            os_ref[...] = _layer_norm(0.5 * os_ref[...], g_ref[...], b_ref[...])


def _ffn_ln(x, xs, w_gate, w_up, w_down, ln_g4, ln_b4, li, si, ln_idx, alpha):
    m, d = x.shape
    ms = xs.shape[0]
    f = w_gate.shape[-1]
    tm = _row_tile(m)
    tf = _pick(f, (512, 256, 128))
    nf = f // tf
    assert nf >= 2 and ms % 16 == 0
    body = functools.partial(_ffn_ln_body, alpha=alpha, nf=nf, tm=tm, ms=ms, col_chunk=_pick(d, (512, 256, 128)))
    return pl.pallas_call(
        body,
        grid=(m // tm, nf),
        in_specs=[
            pl.BlockSpec(memory_space=pl.ANY),
            pl.BlockSpec((ms, d), lambda i, j: (0, 0)),
            pl.BlockSpec((None, None, d, tf), lambda i, j: (li, si, 0, j)),
            pl.BlockSpec((None, None, d, tf), lambda i, j: (li, si, 0, j)),
            pl.BlockSpec((None, None, tf, d), lambda i, j: (li, si, j, 0)),
            pl.BlockSpec((None, None, 1, d), lambda i, j: (li, ln_idx, 0, 0)),
            pl.BlockSpec((None, None, 1, d), lambda i, j: (li, ln_idx, 0, 0)),
        ],
        out_specs=[pl.BlockSpec((tm, d), lambda i, j: (i, 0)), pl.BlockSpec((ms, d), lambda i, j: (0, 0))],
        out_shape=[jax.ShapeDtypeStruct((m, d), F32), jax.ShapeDtypeStruct((ms, d), F32)],
        scratch_shapes=[pltpu.VMEM((tm, d), F32), pltpu.VMEM((tm + ms, d), BF16), pltpu.SemaphoreType.DMA(())],
        compiler_params=_params(("arbitrary", "arbitrary"), VMEM_LIMIT_MAX),
        name="ffn_ln",
    )(x, xs, w_gate, w_up, w_down, ln_g4, ln_b4)


def _resident(block_shape, index_map):
    return pl.BlockSpec(block_shape, index_map, pipeline_mode=pl.Buffered(1))


def _row_chunks(tm):
    rc = min(ROW_CHUNK, tm)
    return [slice(r * rc, (r + 1) * rc) for r in range(tm // rc)]


def _proj_tile(m):
    return _pick(m, (512, 256, 128, 64, 32, 16, 8))


def _ln_specs(li, ln_idx, d):
    spec = _resident((None, None, 1, d), lambda i: (li, ln_idx, 0, 0))
    return [spec, spec]


def _glu_out_ln_body(y_ref, res_ref, wglu_ref, bglu_ref, wout_ref, g_ref, b_ref, o_ref, *, alpha):
    for rows in _row_chunks(o_ref.shape[0]):
        y = y_ref[rows, :]
        t = _dot(y, wglu_ref[...]) + bglu_ref[...]
        z = (y.astype(F32) * jax.nn.sigmoid(t)).astype(BF16)
        o = _dot(z, wout_ref[...])
        o_ref[rows, :] = _layer_norm(alpha * res_ref[rows, :] + o, g_ref[...], b_ref[...])


def _glu_out_ln(y, res, glu_w, glu_b3, w_out, ln_g4, ln_b4, sj, li, ln_idx, alpha):
    m, d = res.shape
    tm = _proj_tile(m)
    return pl.pallas_call(
        functools.partial(_glu_out_ln_body, alpha=alpha),
        grid=(m // tm,),
        in_specs=[
            pl.BlockSpec((tm, d), lambda i: (i, 0)),
            pl.BlockSpec((tm, d), lambda i: (i, 0)),
            _resident((None, d, d), lambda i: (sj, 0, 0)),
            _resident((None, 1, d), lambda i: (sj, 0, 0)),
            _resident((None, d, d), lambda i: (sj, 0, 0)),
            *_ln_specs(li, ln_idx, d),
        ],
        out_specs=pl.BlockSpec((tm, d), lambda i: (i, 0)),
        out_shape=jax.ShapeDtypeStruct((m, d), F32),
        compiler_params=_params(("parallel",)),
        name="s5_glu_out_ln",
    )(y, res, glu_w, glu_b3, w_out, ln_g4, ln_b4)


def _qkv_body(x_ref, w_ref, q_ref, kf_ref, kb_ref, vf_ref, vb_ref, *, scale, n_q, n_k, tn):
    xb = x_ref[...].astype(BF16)
    for c in range(w_ref.shape[1] // tn):
        lo = c * tn
        r = _dot(xb, w_ref[:, lo:lo + tn])
        if lo < n_q:
            q_ref[:, lo:lo + tn] = (r * scale).astype(q_ref.dtype)
        elif lo < n_q + n_k:
            kf_ref[:, lo - n_q:lo - n_q + tn] = r
            kb_ref[:, lo - n_q:lo - n_q + tn] = r.astype(kb_ref.dtype)
        else:
            vf_ref[:, lo - n_q - n_k:lo - n_q - n_k + tn] = r
            vb_ref[:, lo - n_q - n_k:lo - n_q - n_k + tn] = r.astype(vb_ref.dtype)


def _qkv_proj(x, w3, wj, n_q, n_k, n_v, scale):
    m, d = x.shape
    tm = _pick(m, (256, 128, 64, 32, 16, 8))
    tn = _pick(math.gcd(n_q, math.gcd(n_k, n_v)), (512, 256, 128))
    body = functools.partial(_qkv_body, scale=scale, n_q=n_q, n_k=n_k, tn=tn)
    widths_dtypes = [(n_q, BF16), (n_k, F32), (n_k, BF16), (n_v, F32), (n_v, BF16)]
    return pl.pallas_call(
        body,
        grid=(m // tm,),
        in_specs=[
            pl.BlockSpec((tm, d), lambda i: (i, 0)),
            _resident((None, d, n_q + n_k + n_v), lambda i: (wj, 0, 0)),
        ],
        out_specs=[pl.BlockSpec((tm, n), lambda i: (i, 0)) for n, _ in widths_dtypes],
        out_shape=[jax.ShapeDtypeStruct((m, n), dt) for n, dt in widths_dtypes],
        compiler_params=_params(("parallel",)),
        name="qkv_proj",
    )(x, w3)


def _proj_ln_body(x_ref, res_ref, w_ref, g_ref, b_ref, o_ref, *, alpha):
    for rows in _row_chunks(o_ref.shape[0]):
        o = _dot(x_ref[rows, :].astype(BF16), w_ref[...])
        o_ref[rows, :] = _layer_norm(alpha * res_ref[rows, :] + o, g_ref[...], b_ref[...])


def _proj_ln(x, res, w3, wj, ln_g4, ln_b4, li, ln_idx, alpha):
    m, k = x.shape
    d = res.shape[1]
    tm = _proj_tile(m)
    return pl.pallas_call(
        functools.partial(_proj_ln_body, alpha=alpha),
        grid=(m // tm,),
        in_specs=[
            pl.BlockSpec((tm, k), lambda i: (i, 0)),
            pl.BlockSpec((tm, d), lambda i: (i, 0)),
            _resident((None, k, d), lambda i: (wj, 0, 0)),
            *_ln_specs(li, ln_idx, d),
        ],
        out_specs=pl.BlockSpec((tm, d), lambda i: (i, 0)),
        out_shape=jax.ShapeDtypeStruct((m, d), F32),
        compiler_params=_params(("parallel",)),
        name="proj_ln",
    )(x, res, w3, ln_g4, ln_b4)


def _pe_body(h_ref, p_ref, wg_ref, wp_ref, o_ref):
    for rows in _row_chunks(o_ref.shape[0]):
        h = h_ref[rows, :]
        gate = _dot(h.astype(BF16), wg_ref[...])
        pe = _dot(p_ref[rows, :].astype(BF16), wp_ref[...])
        o_ref[rows, :] = h + jax.nn.sigmoid(gate) * pe


def _pe_gate(h, p3, w_pe, w_pe_gate, li):
    m, d = h.shape
    pd = p3.shape[-1]
    tm = _proj_tile(m)
    return pl.pallas_call(
        _pe_body,
        grid=(m // tm,),
        in_specs=[
            pl.BlockSpec((tm, d), lambda i: (i, 0)),
            pl.BlockSpec((None, tm, pd), lambda i: (li, i, 0)),
            _resident((None, d, d), lambda i: (li, 0, 0)),
            _resident((None, pd, d), lambda i: (li, 0, 0)),
        ],
        out_specs=pl.BlockSpec((tm, d), lambda i: (i, 0)),
        out_shape=jax.ShapeDtypeStruct((m, d), F32),
        compiler_params=_params(("parallel",)),
        name="pe_gate",
    )(h, p3, w_pe_gate, w_pe)


def _cmul_add(sr, si, ar, ai, pr, pi):
    return sr + ar * pr - ai * pi, si + ar * pi + ai * pr


def _sublane_scan(sr, si, coef_ref, first, nlog, re, im):
    for k in range(nlog):
        ar, ai = coef_ref[first + k, :, re], coef_ref[first + k, :, im]
        pr = pltpu.roll(sr, 1 << k, axis=0)
        pi = pltpu.roll(si, 1 << k, axis=0)
        sr, si = _cmul_add(sr, si, ar, ai, pr, pi)
    return sr, si


def _s5_in_proj(x_ref, bblk_ref, bu_ref, rows, row_block):
    for rb in range(rows // row_block):
        r = slice(rb * row_block, (rb + 1) * row_block)
        bu_ref[r, :] = _dot(x_ref[r, :].astype(BF16), bblk_ref[...])


def _s5_out_proj(x_ref, s_ref, cblk_ref, d_ref, y_ref, rows, row_block):
    dvec = d_ref[...]
    for rb in range(rows // row_block):
        r = slice(rb * row_block, (rb + 1) * row_block)
        y = _dot(s_ref[r, :].astype(BF16), cblk_ref[...]) + dvec * x_ref[r, :]
        y_ref[r, :] = jax.nn.gelu(y).astype(y_ref.dtype)


def _s5_short_body(x_ref, bblk_ref, cblk_ref, coef_ref, d_ref, s0_ref, y_ref, st_ref, bu_ref, *, rows, nlog,
                   row_block):
    ns = bu_ref.shape[1] // 2
    re, im = slice(0, ns), slice(ns, 2 * ns)
    _s5_in_proj(x_ref, bblk_ref, bu_ref, rows, row_block)
    ar, ai = coef_ref[nlog, 0:1, re], coef_ref[nlog, 0:1, im]
    br, bi = _cmul_add(bu_ref[:, re], bu_ref[:, im], ar, ai, s0_ref[:, re], s0_ref[:, im])
    bu_ref[:, re] = br
    bu_ref[:, im] = bi

    def group(gi, c):
        rws = pl.ds(pl.multiple_of(gi * SUBLANES, SUBLANES), SUBLANES)
        sr, si = _sublane_scan(bu_ref[rws, re], bu_ref[rws, im], coef_ref, 0, nlog, re, im)
        bu_ref[rws, re] = sr
        bu_ref[rws, im] = si
        return c

    lax.fori_loop(0, rows // SUBLANES, group, 0)
    st_ref[...] = bu_ref[...]
    _s5_out_proj(x_ref, bu_ref, cblk_ref, d_ref, y_ref, rows, row_block)


def _s5_long_body(x_ref, bblk_ref, cblk_ref, coef_ref, d_ref, y_ref, st_ref, bu_ref, pw_ref, *, rows, row_block):
    ns = bu_ref.shape[1] // 2
    re, im = slice(0, ns), slice(ns, 2 * ns)
    seg_len = rows // SUBLANES
    _s5_in_proj(x_ref, bblk_ref, bu_ref, rows, row_block)

    ar, ai = coef_ref[0, :, re], coef_ref[0, :, im]

    def step(i, c):
        rws = pl.ds(pl.multiple_of(i * SUBLANES, SUBLANES), SUBLANES)
        sr, si = _cmul_add(bu_ref[rws, re], bu_ref[rws, im], ar, ai, c[0], c[1])
        bu_ref[rws, re] = sr
        bu_ref[rws, im] = si
        return sr, si

    zero = jnp.zeros((SUBLANES, ns), F32)
    er, ei = lax.fori_loop(0, seg_len, step, (zero, zero), unroll=4)

    er, ei = _sublane_scan(er, ei, coef_ref, 1, 3, re, im)
    st_ref[:, re] = er[SUBLANES - 1:SUBLANES, :]
    st_ref[:, im] = ei[SUBLANES - 1:SUBLANES, :]
    first = lax.broadcasted_iota(jnp.int32, (SUBLANES, ns), 0) == 0
    br = jnp.where(first, 0.0, pltpu.roll(er, 1, axis=0))
    bi = jnp.where(first, 0.0, pltpu.roll(ei, 1, axis=0))

    pw_ref[0:SUBLANES, :] = coef_ref[4]
    n = SUBLANES
    while n < seg_len:
        lr, li = pw_ref[n - 1:n, re], pw_ref[n - 1:n, im]
        qr, qi = pw_ref[0:n, re], pw_ref[0:n, im]
        pw_ref[n:2 * n, re] = qr * lr - qi * li
        pw_ref[n:2 * n, im] = qr * li + qi * lr
        n *= 2

    def fix(i, c):
        rws = pl.ds(pl.multiple_of(i * SUBLANES, SUBLANES), SUBLANES)
        sr, si = _cmul_add(bu_ref[rws, re], bu_ref[rws, im], pw_ref[pl.ds(i, 1), re], pw_ref[pl.ds(i, 1), im],
                           br, bi)
        bu_ref[rws, re] = sr
        bu_ref[rws, im] = si
        return c

    lax.fori_loop(0, seg_len, fix, 0, unroll=4)
    _s5_out_proj(x_ref, bu_ref, cblk_ref, d_ref, y_ref, rows, row_block)


def _s5_tables(lam_re, lam_im, log_dt, b_re, b_im, c_re, c_im, seg=None, long_len=None):
    g, p = lam_re.shape
    w = b_re.shape[-1]
    gpc = S5_CHUNK // w
    nc = g // gpc
    dt = jnp.exp(log_dt)[:, None]
    mag = jnp.exp(lam_re * dt)
    ang = lam_im * dt
    a_re, a_im = mag * jnp.cos(ang), mag * jnp.sin(ang)
    den = lam_re * lam_re + lam_im * lam_im
    coef_re = ((a_re - 1.0) * lam_re + a_im * lam_im) / den
    coef_im = (a_im * lam_re - (a_re - 1.0) * lam_im) / den
    bb_re = coef_re[..., None] * b_re - coef_im[..., None] * b_im
    bb_im = coef_re[..., None] * b_im + coef_im[..., None] * b_re
    eye = jnp.eye(gpc, dtype=F32)

    def blockdiag_in(bb):
        t = bb.reshape(nc, gpc, p, w)
        t = jnp.einsum('cgpw,gh->cgwhp', t, eye)
        return t.reshape(nc, gpc * w, gpc * p)

    def blockdiag_out(cc):
        t = cc.reshape(nc, gpc, w, p)
        t = jnp.einsum('cgwp,gh->cgphw', t, eye)
        return t.reshape(nc, gpc * p, gpc * w)

    bblk = jnp.concatenate([blockdiag_in(bb_re), blockdiag_in(bb_im)], axis=-1).astype(BF16)
    cblk = jnp.concatenate([blockdiag_out(c_re), -blockdiag_out(c_im)], axis=1).astype(BF16)

    def rows_of(zr, zi):
        return jnp.concatenate([zr.reshape(nc, gpc * p), zi.reshape(nc, gpc * p)], axis=-1)

    def csq(zr, zi):
        return zr * zr - zi * zi, 2.0 * zr * zi

    def on_rows(zr, zi):
        z = rows_of(zr, zi)
        return jnp.broadcast_to(z[:, None, :], z.shape[:1] + (SUBLANES,) + z.shape[1:])

    def scan_rows(zr, zi, seg, nlog):
        row = jnp.arange(SUBLANES)
        out = []
        for k in range(nlog):
            keep = ((row % seg) >= (1 << k)).astype(F32)[None, :, None]
            out.append(on_rows(zr, zi) * keep)
            zr, zi = csq(zr, zi)
        return out

    if long_len is None:
        nlog = int(math.log2(seg))
        tabs = scan_rows(a_re, a_im, seg, nlog) + [on_rows(a_re, a_im)]
    else:
        nlog = 3
        sr, si = a_re, a_im
        for _ in range(int(math.log2(long_len // SUBLANES))):
            sr, si = csq(sr, si)
        pws = []
        qr, qi = a_re, a_im
        for _ in range(SUBLANES):
            pws.append(rows_of(qr, qi))
            qr, qi = qr * a_re - qi * a_im, qr * a_im + qi * a_re
        tabs = [on_rows(a_re, a_im)] + scan_rows(sr, si, SUBLANES, nlog) + [jnp.stack(pws, axis=1)]
    coef = jnp.stack(tabs, axis=1)
    return bblk, cblk, coef, nlog


def _s5_core(h3, tables, d_skip2, sj, s0_rows=None):
    bblk, cblk, coef, nlog = tables
    nb, rows, d = h3.shape
    nc = d // S5_CHUNK
    ns2 = bblk.shape[-1]
    ntab = coef.shape[1]
    has_s0 = s0_rows is not None
    row_block = _pick(rows, (256, 128, 64, 32, 16, 8))
    scratch = [pltpu.VMEM((rows, ns2), F32)]
    if has_s0:
        body = functools.partial(_s5_short_body, rows=rows, nlog=nlog, row_block=row_block)
    else:
        body = functools.partial(_s5_long_body, rows=rows, row_block=row_block)
        scratch.append(pltpu.VMEM((rows // SUBLANES, ns2), F32))
    in_specs = [
        pl.BlockSpec((None, rows, S5_CHUNK), lambda b, c: (b, 0, c)),
        pl.BlockSpec((None, S5_CHUNK, ns2), lambda b, c: (c, 0, 0)),
        pl.BlockSpec((None, ns2, S5_CHUNK), lambda b, c: (c, 0, 0)),
        pl.BlockSpec((None, ntab, SUBLANES, ns2), lambda b, c: (c, 0, 0, 0)),
        pl.BlockSpec((None, 1, S5_CHUNK), lambda b, c: (sj, 0, c)),
    ]
    args = [h3, bblk, cblk, coef, d_skip2]
    st_rows = rows if has_s0 else 1
    if has_s0:
        in_specs.append(pl.BlockSpec((None, None, rows, ns2), lambda b, c: (b, c, 0, 0)))
        args.append(s0_rows)
    return pl.pallas_call(
        body,
        grid=(nb, nc),
        in_specs=in_specs,
        out_specs=[
            pl.BlockSpec((None, rows, S5_CHUNK), lambda b, c: (b, 0, c)),
            pl.BlockSpec((None, None, st_rows, ns2), lambda b, c: (b, c, 0, 0)),
        ],
        out_shape=[
            jax.ShapeDtypeStruct((nb, rows, d), BF16),
            jax.ShapeDtypeStruct((nb, nc, st_rows, ns2), F32),
        ],
        scratch_shapes=scratch,
        compiler_params=_params(("parallel", "parallel")),
        name="s5_core",
    )(*args)


def _subln(o, g, out_scale):
    return o * lax.rsqrt(jnp.mean(o * o, axis=-1, keepdims=True) + LN_EPS) * g * out_scale


def _online_softmax_step(s, c, m, l, acc_ref, v):
    m_new = jnp.maximum(m, jnp.max(s, axis=-1, keepdims=True) + c)
    corr = jnp.exp(m - m_new)
    p = jnp.exp(s - (m_new - c))
    l_new = l * corr + jnp.sum(p, axis=-1, keepdims=True)
    acc_ref[...] = acc_ref[...] * corr + _dot(p.astype(BF16), v)
    return m_new, l_new


def _attn_prompt_body(slopes_ref, lam_ref, q_ref, k_ref, v_ref, g_ref, o_ref, bias_ref, s_ref, *, tq, nq, half,
                      out_scale):
    h = pl.program_id(1)
    neg_slope = -LOG2_E * slopes_ref[h]
    lam = lam_ref[0]
    rel = (lax.broadcasted_iota(jnp.int32, (tq, tq), 0) - lax.broadcasted_iota(jnp.int32, (tq, tq), 1)).astype(F32)
    bias_ref[0] = rel * neg_slope
    bias_ref[1] = jnp.where(rel >= 0, rel * neg_slope, NEG_INF)
    lane = lax.broadcasted_iota(jnp.int32, (tq, q_ref.shape[1]), 1)

    slot = 0
    for qi in range(nq):
        q = q_ref[qi * tq:(qi + 1) * tq, :]
        q1 = jnp.where(lane < half, q, jnp.zeros_like(q))
        q2 = jnp.where(lane >= half, q, jnp.zeros_like(q))
        offs = [neg_slope * float((qi - kj) * tq) for kj in range(qi + 1)]
        m1 = m2 = None
        for kj in range(qi + 1):
            kb = k_ref[kj * tq:(kj + 1) * tq, :]
            bias = bias_ref[1 if kj == qi else 0]
            s1 = _dot_nt(q1, kb) + bias
            s2 = _dot_nt(q2, kb) + bias
            s_ref[0, slot + kj] = s1
            s_ref[1, slot + kj] = s2
            x1 = jnp.max(s1, axis=-1, keepdims=True) + offs[kj]
            x2 = jnp.max(s2, axis=-1, keepdims=True) + offs[kj]
            m1 = x1 if m1 is None else jnp.maximum(m1, x1)
            m2 = x2 if m2 is None else jnp.maximum(m2, x2)
        acc1 = acc2 = l1 = l2 = None
        for kj in range(qi + 1):
            vb = v_ref[kj * tq:(kj + 1) * tq, :]
            p1 = jnp.exp2(s_ref[0, slot + kj] - (m1 - offs[kj]))
            p2 = jnp.exp2(s_ref[1, slot + kj] - (m2 - offs[kj]))
            d1 = _dot(p1.astype(BF16), vb)
            d2 = _dot(p2.astype(BF16), vb)
            r1 = jnp.sum(p1, axis=-1, keepdims=True)
            r2 = jnp.sum(p2, axis=-1, keepdims=True)
            acc1, acc2 = (d1, d2) if acc1 is None else (acc1 + d1, acc2 + d2)
            l1, l2 = (r1, r2) if l1 is None else (l1 + r1, l2 + r2)
        slot += qi + 1
        o = acc1 / l1 - lam * (acc2 / l2)
        o_ref[qi * tq:(qi + 1) * tq, :] = _subln(o, g_ref[...], out_scale).astype(o_ref.dtype)


def _attn_prompt(q, k, v, slopes, lam, subln_g3, aj, nb, seq, n_heads, out_scale):
    m, hd = q.shape
    dh = hd // n_heads
    tq = _pick(seq, (512, 256, 128, 64, 32, 16, 8))
    nq = seq // tq
    body = functools.partial(_attn_prompt_body, tq=tq, nq=nq, half=dh // 2, out_scale=out_scale)
    return pl.pallas_call(
        body,
        grid=(nb, n_heads),
        in_specs=[
            pl.BlockSpec(memory_space=pltpu.SMEM),
            pl.BlockSpec(memory_space=pltpu.SMEM),
            pl.BlockSpec((seq, dh), lambda b, h: (b, h)),
            pl.BlockSpec((seq, dh), lambda b, h: (b, h)),
            pl.BlockSpec((seq, dh), lambda b, h: (b, h)),
            pl.BlockSpec((None, 1, dh), lambda b, h: (aj, 0, 0)),
        ],
        out_specs=pl.BlockSpec((seq, dh), lambda b, h: (b, h)),
        out_shape=jax.ShapeDtypeStruct((m, hd), BF16),
        scratch_shapes=[pltpu.VMEM((2, tq, tq), F32), pltpu.VMEM((2, nq * (nq + 1) // 2, tq, tq), F32)],
        compiler_params=_params(("parallel", "parallel"), VMEM_LIMIT_SMALL),
        name="diff_attn_prompt",
    )(slopes, lam, q, k, v, subln_g3)


def _attn_decode_body(pt_ref, lam_ref, qc_ref, *refs, n_pages, page, gp, n_heads, t_new, out_scale):
    k_refs, v_refs = refs[:gp], refs[gp:2 * gp]
    (kn_ref, vn_ref, cmat_ref, cnew_ref, slope_ref, step_ref, g_ref, o_ref, acc_ref, m_ref, l_ref,
     s_ref) = refs[2 * gp:]
    p = pl.program_id(1)
    past = n_pages * page

    @pl.when(p == 0)
    def _():
        m_ref[...] = jnp.full_like(m_ref, NEG_INF)
        l_ref[...] = jnp.zeros_like(l_ref)
        acc_ref[...] = jnp.zeros_like(acc_ref)

    qc = qc_ref[...]
    slope = slope_ref[...]
    qpos = step_ref[...] + float(past)

    m_old = m_ref[...]
    m_new = m_old
    offs = []
    for g in range(gp):
        s = _dot_nt(qc, k_refs[g][...].astype(BF16)) + cmat_ref[...]
        s_ref[g] = s
        c = slope * (((p * gp + g) * page).astype(F32) - qpos)
        offs.append(c)
        m_new = jnp.maximum(m_new, jnp.max(s, axis=-1, keepdims=True) + c)
    lsum = jnp.zeros_like(m_new)
    pv = jnp.zeros(acc_ref.shape, F32)
    for g in range(gp):
        pe = jnp.exp(s_ref[g] - (m_new - offs[g]))
        lsum = lsum + jnp.sum(pe, axis=-1, keepdims=True)
        pv = pv + _dot(pe.astype(BF16), v_refs[g][...].astype(BF16))
    corr = jnp.exp(m_old - m_new)
    m_ref[...] = m_new
    l_ref[...] = l_ref[...] * corr + lsum
    acc_ref[...] = acc_ref[...] * corr + pv

    @pl.when(p == n_pages // gp - 1)
    def _():
        s = _dot_nt(qc, kn_ref[...]) + cnew_ref[...]
        m_fin, l_fin = _online_softmax_step(s, 0.0, m_ref[...], l_ref[...], acc_ref, vn_ref[...])
        lam = lam_ref[0]
        inv = 1.0 / l_fin
        rph = acc_ref.shape[0] // n_heads
        dh = acc_ref.shape[1]
        for h in range(n_heads):
            blk = acc_ref[h * rph:(h + 1) * rph, :] * inv[h * rph:(h + 1) * rph, :]
            o = blk[0:t_new, :] - lam * blk[rph // 2:rph // 2 + t_new, :]
            o_ref[:, h * dh:(h + 1) * dh] = _subln(o, g_ref[...], out_scale)


def _attn_decode(q, k_new, v_new, cache_k4, cache_v4, page_table, slopes, lam, subln_g3, aj, nb, t_new, n_heads,
                 out_scale):
    hd = q.shape[1]
    dh = hd // n_heads
    half = dh // 2
    prows = cache_k4.shape[2]
    page = prows // n_heads
    n_pages = page_table.shape[1]
    gp = _pick(n_pages, (8, 4, 2, 1))
    rph = SUBLANES
    tslots = SUBLANES
    assert t_new <= rph // 2 and t_new <= tslots
    nr = n_heads * rph
    q4 = jnp.transpose(q.reshape(nb, t_new, n_heads, dh), (0, 2, 1, 3))
    q4 = jnp.pad(q4, ((0, 0), (0, 0), (0, rph // 2 - t_new), (0, 0)))
    lane = jnp.arange(dh)
    qc = jnp.stack([jnp.where(lane < half, q4, 0), jnp.where(lane >= half, q4, 0)], axis=2).reshape(nb, nr, dh)
    pad_tok = ((0, 0), (0, tslots - t_new), (0, 0), (0, 0))
    kn = jnp.pad(k_new.reshape(nb, t_new, n_heads, dh), pad_tok).reshape(nb, tslots * n_heads, dh)
    vn = jnp.pad(v_new.reshape(nb, t_new, n_heads, dh), pad_tok).reshape(nb, tslots * n_heads, dh)
    row = jnp.arange(nr)
    row_head, row_step = row // rph, (row % (rph // 2)).astype(F32)
    slope_rows = slopes[row_head][:, None]
    step_rows = row_step[:, None]
    col = jnp.arange(prows)
    cmat = jnp.where((col % n_heads)[None, :] == row_head[:, None],
                     slope_rows * (col // n_heads).astype(F32)[None, :], NEG_INF)
    coln = jnp.arange(tslots * n_heads)
    tokn = (coln // n_heads).astype(F32)[None, :]
    ok = jnp.logical_and((coln % n_heads)[None, :] == row_head[:, None],
                         jnp.logical_and(tokn < t_new, tokn <= step_rows))
    cnew = jnp.where(ok, -slope_rows * (step_rows - tokn), NEG_INF)
    body = functools.partial(_attn_decode_body, n_pages=n_pages, page=page, gp=gp, n_heads=n_heads, t_new=t_new,
                             out_scale=out_scale)

    def page_spec(g):
        return pl.BlockSpec((None, None, prows, dh), lambda b, p, pt: (aj, pt[b * n_pages + p * gp + g], 0, 0))

    grid_spec = pltpu.PrefetchScalarGridSpec(
        num_scalar_prefetch=1,
        grid=(nb, n_pages // gp),
        in_specs=[
            pl.BlockSpec(memory_space=pltpu.SMEM),
            pl.BlockSpec((None, nr, dh), lambda b, p, pt: (b, 0, 0)),
            *[page_spec(g) for g in range(gp)],
            *[page_spec(g) for g in range(gp)],
            pl.BlockSpec((None, tslots * n_heads, dh), lambda b, p, pt: (b, 0, 0)),
            pl.BlockSpec((None, tslots * n_heads, dh), lambda b, p, pt: (b, 0, 0)),
            pl.BlockSpec((nr, prows), lambda b, p, pt: (0, 0)),
            pl.BlockSpec((nr, tslots * n_heads), lambda b, p, pt: (0, 0)),
            pl.BlockSpec((nr, 1), lambda b, p, pt: (0, 0)),
            pl.BlockSpec((nr, 1), lambda b, p, pt: (0, 0)),
            pl.BlockSpec((None, 1, dh), lambda b, p, pt: (aj, 0, 0)),
        ],
        out_specs=pl.BlockSpec((None, t_new, hd), lambda b, p, pt: (b, 0, 0)),
        scratch_shapes=[pltpu.VMEM((nr, dh), F32), pltpu.VMEM((nr, 1), F32), pltpu.VMEM((nr, 1), F32),
                        pltpu.VMEM((gp, nr, prows), F32)],
    )
    return pl.pallas_call(
        body,
        grid_spec=grid_spec,
        out_shape=jax.ShapeDtypeStruct((nb, t_new, hd), F32),
        compiler_params=_params(("parallel", "arbitrary")),
        name="diff_attn_decode",
    )(page_table.reshape(-1), lam, qc, *([cache_k4] * gp), *([cache_v4] * gp), kn, vn, cmat, cnew, slope_rows,
      step_rows, subln_g3)


def kernel(x_prompt, x_sample, cache_k, cache_v, state_s5_re, state_s5_im, page_table, p_prompt, p_sample,
           ln_g, ln_b, w_ffn_gate, w_ffn_up, w_ffn_down, w_pe, w_pe_gate,
           s5_lambda_re, s5_lambda_im, s5_log_dt, s5_b_re, s5_b_im, s5_c_re, s5_c_im, s5_d,
           s5_glu_w, s5_glu_b, s5_w_out,
           attn_w_qkv, attn_lambda_q1, attn_lambda_k1, attn_lambda_q2, attn_lambda_k2, attn_subln_g, attn_w_o):
    depth = ln_g.shape[0]
    alpha = (2.0 * depth) ** 0.25
    nbp, seq, d = x_prompt.shape
    nbs, t_new, _ = x_sample.shape
    n_attn, n_pool, page, n_heads, qk_dim = cache_k.shape
    v_dim = cache_v.shape[-1]
    half = qk_dim // 2
    attn_scale = 1.0 / math.sqrt(half)
    g_cnt, p_cnt = s5_lambda_re.shape[1:]
    gw = s5_b_re.shape[-1]
    gpc = S5_CHUNK // gw
    nc = d // S5_CHUNK
    hq = n_heads * qk_dim

    ln_g4 = ln_g.reshape(depth, ln_g.shape[1], 1, d)
    ln_b4 = ln_b.reshape(depth, ln_b.shape[1], 1, d)
    pp3 = p_prompt.reshape(depth, nbp * seq, -1)
    ps3 = p_sample.reshape(depth, nbs * t_new, -1)
    cache_k4 = cache_k.reshape(n_attn, n_pool, page * n_heads, qk_dim)
    cache_v4 = cache_v.reshape(n_attn, n_pool, page * n_heads, v_dim)
    attn_w_qkv = attn_w_qkv.astype(BF16)
    attn_w_o = attn_w_o.astype(BF16)
    s5_glu_w = s5_glu_w.astype(BF16)
    s5_w_out = s5_w_out.astype(BF16)
    w_pe_gate = w_pe_gate.astype(BF16)
    w_pe = w_pe.astype(BF16)
    glu_b3 = s5_glu_b.reshape(s5_glu_b.shape[0], 1, d)
    d_skip3 = s5_d.reshape(s5_d.shape[0], 1, d)
    subln_g3 = attn_subln_g.reshape(n_attn, 1, v_dim)
    slopes = jnp.exp2(-8.0 * jnp.arange(1, n_heads + 1, dtype=F32) / n_heads)

    hp = x_prompt.reshape(nbp * seq, d)
    hs = x_sample.reshape(nbs * t_new, d)
    kp, vp, ksm, vsm = [], [], [], []
    srp, sip, srs, sis = [], [], [], []

    def split_state(st):
        lead = st.shape[:-2]
        st = st.reshape(lead + (nc, 2, gpc, p_cnt))
        return (st[..., 0, :, :].reshape(lead + (g_cnt, p_cnt)), st[..., 1, :, :].reshape(lead + (g_cnt, p_cnt)))

    for i in range(depth):
        j = i // 2
        hp, hs = _ffn_ln(hp, hs, w_ffn_gate, w_ffn_up, w_ffn_down, ln_g4, ln_b4, i, 0, 0, alpha)
        if i % 2 == 0:
            prm = (s5_lambda_re[j], s5_lambda_im[j], s5_log_dt[j], s5_b_re[j], s5_b_im[j], s5_c_re[j], s5_c_im[j])
            tabs_p = _s5_tables(*prm, long_len=seq)
            seg_len = seq // SUBLANES
            hperm = jnp.transpose(hp.reshape(nbp, SUBLANES, seg_len, d), (0, 2, 1, 3)).reshape(nbp, seq, d)
            yperm, stp = _s5_core(hperm, tabs_p, d_skip3, j)
            yp = jnp.transpose(yperm.reshape(nbp, seg_len, SUBLANES, d), (0, 2, 1, 3)).reshape(nbp * seq, d)
            re, im = split_state(stp[:, :, 0, :])
            srp.append(re)
            sip.append(im)
            hp = _glu_out_ln(yp, hp, s5_glu_w, glu_b3, s5_w_out, ln_g4, ln_b4, j, i, 1, alpha)
            tabs_s = _s5_tables(*prm, seg=t_new)
            s0 = jnp.concatenate([state_s5_re[j].reshape(nbs, nc, gpc * p_cnt),
                                  state_s5_im[j].reshape(nbs, nc, gpc * p_cnt)], axis=-1)
            s0_rows = jnp.zeros((nbs, t_new, nc, s0.shape[-1]), F32).at[:, 0].set(s0)
            s0_rows = jnp.transpose(s0_rows.reshape(nbs * t_new, nc, -1), (1, 0, 2))[None]
            ys, sts = _s5_core(hs.reshape(1, nbs * t_new, d), tabs_s, d_skip3, j, s0_rows)
            st_last = sts[0].reshape(nc, nbs, t_new, -1)[:, :, t_new - 1, :]
            re, im = split_state(jnp.transpose(st_last, (1, 0, 2)))
            srs.append(re)
            sis.append(im)
            hs = _glu_out_ln(ys.reshape(nbs * t_new, d), hs, s5_glu_w, glu_b3, s5_w_out, ln_g4, ln_b4, j, i, 1,
                             alpha)
        else:
            lam_init = 0.8 - 0.6 * math.exp(-0.3 * i)
            lam = (jnp.exp(jnp.sum(attn_lambda_q1[j] * attn_lambda_k1[j]))
                   - jnp.exp(jnp.sum(attn_lambda_q2[j] * attn_lambda_k2[j])) + lam_init).reshape(1).astype(F32)
            out_scale = 1.0 - lam_init
            qp, kf, kb, vf, vb = _qkv_proj(hp, attn_w_qkv, j, hq, hq, n_heads * v_dim, attn_scale * LOG2_E)
            kp.append(kf.reshape(nbp, seq, n_heads, qk_dim))
            vp.append(vf.reshape(nbp, seq, n_heads, v_dim))
            op = _attn_prompt(qp, kb, vb, slopes, lam, subln_g3, j, nbp, seq, n_heads, out_scale)
            hp = _proj_ln(op, hp, attn_w_o, j, ln_g4, ln_b4, i, 1, alpha)
            qs, kf, kb, vf, vb = _qkv_proj(hs, attn_w_qkv, j, hq, hq, n_heads * v_dim, attn_scale)
            ksm.append(kf.reshape(nbs, t_new, n_heads, qk_dim))
            vsm.append(vf.reshape(nbs, t_new, n_heads, v_dim))
            osm = _attn_decode(qs, kb, vb, cache_k4, cache_v4, page_table, slopes, lam, subln_g3, j, nbs, t_new,
                               n_heads, out_scale)
            hs = _proj_ln(osm.reshape(nbs * t_new, n_heads * v_dim), hs, attn_w_o, j, ln_g4, ln_b4, i, 1, alpha)
        hp, hs = _ffn_ln(hp, hs, w_ffn_gate, w_ffn_up, w_ffn_down, ln_g4, ln_b4, i, 1, 2, alpha)
        hp = _pe_gate(hp, pp3, w_pe, w_pe_gate, i)
        hs = _pe_gate(hs, ps3, w_pe, w_pe_gate, i)

    return (hp.reshape(nbp, seq, d), hs.reshape(nbs, t_new, d),
            jnp.stack(kp), jnp.stack(vp), jnp.stack(ksm), jnp.stack(vsm),
            jnp.stack(srp), jnp.stack(sip), jnp.stack(srs), jnp.stack(sis))
```

```python
import functools
import math

import jax
import jax.numpy as jnp
from jax import lax
from jax.experimental import pallas as pl
from jax.experimental.pallas import tpu as pltpu

F32 = jnp.float32
BF16 = jnp.bfloat16
LN_EPS = 1e-5
NEG_INF = -1e30
LOG2_E = math.log2(math.e)
LANES = 128
S5_CHUNK = 2 * LANES
SUBLANES = 8
VMEM_LIMIT_MAX = 60 * 1024 * 1024
VMEM_LIMIT_BIG = 56 * 1024 * 1024
VMEM_LIMIT_SMALL = 40 * 1024 * 1024
EPILOGUE_ROWS = 128
ROW_CHUNK = 128


def _pick(n, cands):
    for c in cands:
        if n % c == 0:
            return c
    return n


def _row_tile(m):
    return _pick(m, (1024, 512, 256, 128, 64, 32, 16, 8))


def _params(sem, vmem=VMEM_LIMIT_BIG):
    return pltpu.CompilerParams(dimension_semantics=sem, vmem_limit_bytes=vmem)


def _dot(a, b):
    return jnp.dot(a, b, preferred_element_type=F32)


def _dot_nt(a, b):
    return lax.dot_general(a, b, (((1,), (1,)), ((), ())), preferred_element_type=F32)


def _layer_norm(z, g, b):
    mu = jnp.mean(z, axis=-1, keepdims=True)
    zc = z - mu
    var = jnp.mean(zc * zc, axis=-1, keepdims=True)
    return zc * lax.rsqrt(var + LN_EPS) * g + b


def _ln_epilogue(o_ref, res_ref, g_ref, b_ref, alpha, scale):
    tm = o_ref.shape[0]
    rb = min(EPILOGUE_ROWS, tm)
    g = g_ref[...]
    b = b_ref[...]

    def chunk(r, carry):
        rows = pl.ds(pl.multiple_of(r * rb, rb), rb)
        z = o_ref[rows, :] if scale == 1.0 else scale * o_ref[rows, :]
        if res_ref is not None:
            z = z + alpha * res_ref[rows, :].astype(F32)
        o_ref[rows, :] = _layer_norm(z, g, b)
        return carry

    lax.fori_loop(0, tm // rb, chunk, 0)


def _ffn_ln_body(x_hbm, xs_ref, wg_ref, wu_ref, wd_ref, g_ref, b_ref, o_ref, os_ref, xin_ref, xb_ref, sem, *,
                 alpha, nf, tm, ms, col_chunk):
    i = pl.program_id(0)
    j = pl.program_id(1)
    n_tiles = pl.num_programs(0)
    d = o_ref.shape[1]
    rb = min(EPILOGUE_ROWS, tm)

    def x_copy(tile):
        return pltpu.make_async_copy(x_hbm.at[pl.ds(tile * tm, tm), :], xin_ref, sem)

    @pl.when(j == 0)
    def _():
        @pl.when(i == 0)
        def _():
            x_copy(0).start()
            xs = xs_ref[...]
            xb_ref[tm:tm + ms, :] = xs.astype(BF16)
            os_ref[...] = (2.0 * alpha) * xs

        x_copy(i).wait()

        def chunk(r, carry):
            rows = pl.ds(pl.multiple_of(r * rb, rb), rb)
            xr = xin_ref[rows, :]
            xb_ref[rows, :] = xr.astype(BF16)
            o_ref[rows, :] = (2.0 * alpha) * xr
            return carry

        lax.fori_loop(0, tm // rb, chunk, 0)

    @pl.when(jnp.logical_and(j == 1, i + 1 < n_tiles))
    def _():
        x_copy(i + 1).start()

    def hidden(nrows):
        xb = xb_ref[0:nrows, :]
        gate = _dot(xb, wg_ref[...].astype(BF16))
        up = _dot(xb, wu_ref[...].astype(BF16))
        return (gate * jax.nn.sigmoid(gate) * up).astype(BF16)

    def step(nrows):
        h = hidden(nrows)
        for c in range(d // col_chunk):
            cols = slice(c * col_chunk, (c + 1) * col_chunk)
            r = _dot(h, wd_ref[:, cols].astype(BF16))
            o_ref[:, cols] += r[0:tm, :]
            if nrows > tm:
                os_ref[:, cols] += r[tm:nrows, :]

    def last_step(nrows):
        h = hidden(nrows)
        wd = wd_ref[...].astype(BF16)
        g, b = g_ref[...], b_ref[...]
        for rows in _row_chunks(tm):
            z = 0.5 * (o_ref[rows, :] + _dot(h[rows, :], wd))
            o_ref[rows, :] = _layer_norm(z, g, b)
        if nrows > tm:
            z = 0.5 * (os_ref[...] + _dot(h[tm:nrows, :], wd))
            os_ref[...] = _layer_norm(z, g, b)

    last = j == nf - 1
    first_tile = i == 0
    for is_last, is_first, fn, nrows in ((False, True, step, tm + ms), (False, False, step, tm),
                                         (True, True, last_step, tm + ms), (True, False, last_step, tm)):
        cond = jnp.logical_and(last if is_last else jnp.logical_not(last),
                               first_tile if is_first else jnp.logical_not(first_tile))
        pl.when(cond)(functools.partial(fn, nrows))


def _ffn_ln(x, xs, w_gate, w_up, w_down, ln_g4, ln_b4, li, si, ln_idx, alpha):
    m, d = x.shape
    ms = xs.shape[0]
    f = w_gate.shape[-1]
    tm = _row_tile(m)
    tf = _pick(f, (512, 256, 128))
    nf = f // tf
    assert nf >= 2 and ms % 16 == 0
    body = functools.partial(_ffn_ln_body, alpha=alpha, nf=nf, tm=tm, ms=ms, col_chunk=_pick(d, (512, 256, 128)))
    return pl.pallas_call(
        body,
        grid=(m // tm, nf),
        in_specs=[
            pl.BlockSpec(memory_space=pl.ANY),
            pl.BlockSpec((ms, d), lambda i, j: (0, 0)),
            pl.BlockSpec((None, None, d, tf), lambda i, j: (li, si, 0, j)),
            pl.BlockSpec((None, None, d, tf), lambda i, j: (li, si, 0, j)),
            pl.BlockSpec((None, None, tf, d), lambda i, j: (li, si, j, 0)),
            pl.BlockSpec((None, None, 1, d), lambda i, j: (li, ln_idx, 0, 0)),
            pl.BlockSpec((None, None, 1, d), lambda i, j: (li, ln_idx, 0, 0)),
        ],
        out_specs=[pl.BlockSpec((tm, d), lambda i, j: (i, 0)), pl.BlockSpec((ms, d), lambda i, j: (0, 0))],
        out_shape=[jax.ShapeDtypeStruct((m, d), F32), jax.ShapeDtypeStruct((ms, d), F32)],
        scratch_shapes=[pltpu.VMEM((tm, d), F32), pltpu.VMEM((tm + ms, d), BF16), pltpu.SemaphoreType.DMA(())],
        compiler_params=_params(("arbitrary", "arbitrary"), VMEM_LIMIT_MAX),
        name="ffn_ln",
    )(x, xs, w_gate, w_up, w_down, ln_g4, ln_b4)


def _resident(block_shape, index_map):
    return pl.BlockSpec(block_shape, index_map, pipeline_mode=pl.Buffered(1))


def _row_chunks(tm):
    rc = min(ROW_CHUNK, tm)
    return [slice(r * rc, (r + 1) * rc) for r in range(tm // rc)]


def _proj_tile(m):
    return _pick(m, (512, 256, 128, 64, 32, 16, 8))


def _ln_specs(li, ln_idx, d):
    spec = _resident((None, None, 1, d), lambda i: (li, ln_idx, 0, 0))
    return [spec, spec]


def _glu_out_ln_body(y_ref, res_ref, wglu_ref, bglu_ref, wout_ref, g_ref, b_ref, o_ref, *, alpha):
    for rows in _row_chunks(o_ref.shape[0]):
        y = y_ref[rows, :]
        t = _dot(y, wglu_ref[...]) + bglu_ref[...]
        z = (y.astype(F32) * jax.nn.sigmoid(t)).astype(BF16)
        o = _dot(z, wout_ref[...])
        o_ref[rows, :] = _layer_norm(alpha * res_ref[rows, :] + o, g_ref[...], b_ref[...])


def _glu_out_ln(y, res, glu_w, glu_b3, w_out, ln_g4, ln_b4, sj, li, ln_idx, alpha):
    m, d = res.shape
    tm = _proj_tile(m)
    return pl.pallas_call(
        functools.partial(_glu_out_ln_body, alpha=alpha),
        grid=(m // tm,),
        in_specs=[
            pl.BlockSpec((tm, d), lambda i: (i, 0)),
            pl.BlockSpec((tm, d), lambda i: (i, 0)),
            _resident((None, d, d), lambda i: (sj, 0, 0)),
            _resident((None, 1, d), lambda i: (sj, 0, 0)),
            _resident((None, d, d), lambda i: (sj, 0, 0)),
            *_ln_specs(li, ln_idx, d),
        ],
        out_specs=pl.BlockSpec((tm, d), lambda i: (i, 0)),
        out_shape=jax.ShapeDtypeStruct((m, d), F32),
        compiler_params=_params(("parallel",)),
        name="s5_glu_out_ln",
    )(y, res, glu_w, glu_b3, w_out, ln_g4, ln_b4)


def _qkv_body(x_ref, w_ref, q_ref, kf_ref, kb_ref, vf_ref, vb_ref, *, scale, n_q, n_k, tn):
    xb = x_ref[...].astype(BF16)
    for c in range(w_ref.shape[1] // tn):
        lo = c * tn
        r = _dot(xb, w_ref[:, lo:lo + tn])
        if lo < n_q:
            q_ref[:, lo:lo + tn] = (r * scale).astype(q_ref.dtype)
        elif lo < n_q + n_k:
            kf_ref[:, lo - n_q:lo - n_q + tn] = r
            kb_ref[:, lo - n_q:lo - n_q + tn] = r.astype(kb_ref.dtype)
        else:
            vf_ref[:, lo - n_q - n_k:lo - n_q - n_k + tn] = r
            vb_ref[:, lo - n_q - n_k:lo - n_q - n_k + tn] = r.astype(vb_ref.dtype)


def _qkv_proj(x, w3, wj, n_q, n_k, n_v, scale):
    m, d = x.shape
    tm = _pick(m, (256, 128, 64, 32, 16, 8))
    tn = _pick(math.gcd(n_q, math.gcd(n_k, n_v)), (512, 256, 128))
    body = functools.partial(_qkv_body, scale=scale, n_q=n_q, n_k=n_k, tn=tn)
    widths_dtypes = [(n_q, BF16), (n_k, F32), (n_k, BF16), (n_v, F32), (n_v, BF16)]
    return pl.pallas_call(
        body,
        grid=(m // tm,),
        in_specs=[
            pl.BlockSpec((tm, d), lambda i: (i, 0)),
            _resident((None, d, n_q + n_k + n_v), lambda i: (wj, 0, 0)),
        ],
        out_specs=[pl.BlockSpec((tm, n), lambda i: (i, 0)) for n, _ in widths_dtypes],
        out_shape=[jax.ShapeDtypeStruct((m, n), dt) for n, dt in widths_dtypes],
        compiler_params=_params(("parallel",)),
        name="qkv_proj",
    )(x, w3)


def _proj_ln_body(x_ref, res_ref, w_ref, g_ref, b_ref, o_ref, *, alpha):
    for rows in _row_chunks(o_ref.shape[0]):
        o = _dot(x_ref[rows, :].astype(BF16), w_ref[...])
        o_ref[rows, :] = _layer_norm(alpha * res_ref[rows, :] + o, g_ref[...], b_ref[...])


def _proj_ln(x, res, w3, wj, ln_g4, ln_b4, li, ln_idx, alpha):
    m, k = x.shape
    d = res.shape[1]
    tm = _proj_tile(m)
    return pl.pallas_call(
        functools.partial(_proj_ln_body, alpha=alpha),
        grid=(m // tm,),
        in_specs=[
            pl.BlockSpec((tm, k), lambda i: (i, 0)),
            pl.BlockSpec((tm, d), lambda i: (i, 0)),
            _resident((None, k, d), lambda i: (wj, 0, 0)),
            *_ln_specs(li, ln_idx, d),
        ],
        out_specs=pl.BlockSpec((tm, d), lambda i: (i, 0)),
        out_shape=jax.ShapeDtypeStruct((m, d), F32),
        compiler_params=_params(("parallel",)),
        name="proj_ln",
    )(x, res, w3, ln_g4, ln_b4)


def _pe_body(h_ref, p_ref, wg_ref, wp_ref, o_ref):
    for rows in _row_chunks(o_ref.shape[0]):
        h = h_ref[rows, :]
        gate = _dot(h.astype(BF16), wg_ref[...])
        pe = _dot(p_ref[rows, :].astype(BF16), wp_ref[...])
        o_ref[rows, :] = h + jax.nn.sigmoid(gate) * pe


def _pe_gate(h, p3, w_pe, w_pe_gate, li):
    m, d = h.shape
    pd = p3.shape[-1]
    tm = _proj_tile(m)
    return pl.pallas_call(
        _pe_body,
        grid=(m // tm,),
        in_specs=[
            pl.BlockSpec((tm, d), lambda i: (i, 0)),
            pl.BlockSpec((None, tm, pd), lambda i: (li, i, 0)),
            _resident((None, d, d), lambda i: (li, 0, 0)),
            _resident((None, pd, d), lambda i: (li, 0, 0)),
        ],
        out_specs=pl.BlockSpec((tm, d), lambda i: (i, 0)),
        out_shape=jax.ShapeDtypeStruct((m, d), F32),
        compiler_params=_params(("parallel",)),
        name="pe_gate",
    )(h, p3, w_pe_gate, w_pe)


def _cmul_add(sr, si, ar, ai, pr, pi):
    return sr + ar * pr - ai * pi, si + ar * pi + ai * pr


def _sublane_scan(sr, si, coef_ref, first, nlog, re, im):
    for k in range(nlog):
        ar, ai = coef_ref[first + k, :, re], coef_ref[first + k, :, im]
        pr = pltpu.roll(sr, 1 << k, axis=0)
        pi = pltpu.roll(si, 1 << k, axis=0)
        sr, si = _cmul_add(sr, si, ar, ai, pr, pi)
    return sr, si


def _s5_in_proj(x_ref, bblk_ref, bu_ref, rows, row_block):
    for rb in range(rows // row_block):
        r = slice(rb * row_block, (rb + 1) * row_block)
        bu_ref[r, :] = _dot(x_ref[r, :].astype(BF16), bblk_ref[...])


def _s5_out_proj(x_ref, s_ref, cblk_ref, d_ref, y_ref, rows, row_block):
    dvec = d_ref[...]
    for rb in range(rows // row_block):
        r = slice(rb * row_block, (rb + 1) * row_block)
        y = _dot(s_ref[r, :].astype(BF16), cblk_ref[...]) + dvec * x_ref[r, :]
        y_ref[r, :] = jax.nn.gelu(y).astype(y_ref.dtype)


def _s5_short_body(x_ref, bblk_ref, cblk_ref, coef_ref, d_ref, s0_ref, y_ref, st_ref, bu_ref, *, rows, nlog,
                   row_block):
    ns = bu_ref.shape[1] // 2
    re, im = slice(0, ns), slice(ns, 2 * ns)
    _s5_in_proj(x_ref, bblk_ref, bu_ref, rows, row_block)
    ar, ai = coef_ref[nlog, 0:1, re], coef_ref[nlog, 0:1, im]
    br, bi = _cmul_add(bu_ref[:, re], bu_ref[:, im], ar, ai, s0_ref[:, re], s0_ref[:, im])
    bu_ref[:, re] = br
    bu_ref[:, im] = bi

    def group(gi, c):
        rws = pl.ds(pl.multiple_of(gi * SUBLANES, SUBLANES), SUBLANES)
        sr, si = _sublane_scan(bu_ref[rws, re], bu_ref[rws, im], coef_ref, 0, nlog, re, im)
        bu_ref[rws, re] = sr
        bu_ref[rws, im] = si
        return c

    lax.fori_loop(0, rows // SUBLANES, group, 0)
    st_ref[...] = bu_ref[...]
    _s5_out_proj(x_ref, bu_ref, cblk_ref, d_ref, y_ref, rows, row_block)


def _s5_long_body(x_ref, bblk_ref, cblk_ref, coef_ref, d_ref, y_ref, st_ref, bu_ref, pw_ref, *, rows, row_block):
    ns = bu_ref.shape[1] // 2
    re, im = slice(0, ns), slice(ns, 2 * ns)
    seg_len = rows // SUBLANES
    _s5_in_proj(x_ref, bblk_ref, bu_ref, rows, row_block)

    ar, ai = coef_ref[0, :, re], coef_ref[0, :, im]

    def step(i, c):
        rws = pl.ds(pl.multiple_of(i * SUBLANES, SUBLANES), SUBLANES)
        sr, si = _cmul_add(bu_ref[rws, re], bu_ref[rws, im], ar, ai, c[0], c[1])
        bu_ref[rws, re] = sr
        bu_ref[rws, im] = si
        return sr, si

    zero = jnp.zeros((SUBLANES, ns), F32)
    er, ei = lax.fori_loop(0, seg_len, step, (zero, zero), unroll=4)

    er, ei = _sublane_scan(er, ei, coef_ref, 1, 3, re, im)
    st_ref[:, re] = er[SUBLANES - 1:SUBLANES, :]
    st_ref[:, im] = ei[SUBLANES - 1:SUBLANES, :]
    first = lax.broadcasted_iota(jnp.int32, (SUBLANES, ns), 0) == 0
    br = jnp.where(first, 0.0, pltpu.roll(er, 1, axis=0))
    bi = jnp.where(first, 0.0, pltpu.roll(ei, 1, axis=0))

    pw_ref[0:SUBLANES, :] = coef_ref[4]
    n = SUBLANES
    while n < seg_len:
        lr, li = pw_ref[n - 1:n, re], pw_ref[n - 1:n, im]
        qr, qi = pw_ref[0:n, re], pw_ref[0:n, im]
        pw_ref[n:2 * n, re] = qr * lr - qi * li
        pw_ref[n:2 * n, im] = qr * li + qi * lr
        n *= 2

    def fix(i, c):
        rws = pl.ds(pl.multiple_of(i * SUBLANES, SUBLANES), SUBLANES)
        sr, si = _cmul_add(bu_ref[rws, re], bu_ref[rws, im], pw_ref[pl.ds(i, 1), re], pw_ref[pl.ds(i, 1), im],
                           br, bi)
        bu_ref[rws, re] = sr
        bu_ref[rws, im] = si
        return c

    lax.fori_loop(0, seg_len, fix, 0, unroll=4)
    _s5_out_proj(x_ref, bu_ref, cblk_ref, d_ref, y_ref, rows, row_block)


def _s5_tables(lam_re, lam_im, log_dt, b_re, b_im, c_re, c_im, seg=None, long_len=None):
    g, p = lam_re.shape
    w = b_re.shape[-1]
    gpc = S5_CHUNK // w
    nc = g // gpc
    dt = jnp.exp(log_dt)[:, None]
    mag = jnp.exp(lam_re * dt)
    ang = lam_im * dt
    a_re, a_im = mag * jnp.cos(ang), mag * jnp.sin(ang)
    den = lam_re * lam_re + lam_im * lam_im
    coef_re = ((a_re - 1.0) * lam_re + a_im * lam_im) / den
    coef_im = (a_im * lam_re - (a_re - 1.0) * lam_im) / den
    bb_re = coef_re[..., None] * b_re - coef_im[..., None] * b_im
    bb_im = coef_re[..., None] * b_im + coef_im[..., None] * b_re
    def blockdiag(t):
        a, b = t.shape[1:]
        tiled = jnp.tile(t.reshape(nc, gpc * a, b), (1, 1, gpc))
        same = (jnp.arange(gpc * a) // a)[:, None] == (jnp.arange(gpc * b) // b)[None, :]
        return jnp.where(same[None], tiled, 0.0)

    def blockdiag_in(bb):
        return blockdiag(jnp.transpose(bb, (0, 2, 1)))

    def blockdiag_out(cc):
        return blockdiag(jnp.transpose(cc, (0, 2, 1)))

    bblk = jnp.concatenate([blockdiag_in(bb_re), blockdiag_in(bb_im)], axis=-1).astype(BF16)
    cblk = jnp.concatenate([blockdiag_out(c_re), -blockdiag_out(c_im)], axis=1).astype(BF16)

    def rows_of(zr, zi):
        return jnp.concatenate([zr.reshape(nc, gpc * p), zi.reshape(nc, gpc * p)], axis=-1)

    def csq(zr, zi):
        return zr * zr - zi * zi, 2.0 * zr * zi

    def on_rows(zr, zi):
        z = rows_of(zr, zi)
        return jnp.broadcast_to(z[:, None, :], z.shape[:1] + (SUBLANES,) + z.shape[1:])

    def scan_rows(zr, zi, seg, nlog):
        row = jnp.arange(SUBLANES)
        out = []
        for k in range(nlog):
            keep = ((row % seg) >= (1 << k)).astype(F32)[None, :, None]
            out.append(on_rows(zr, zi) * keep)
            zr, zi = csq(zr, zi)
        return out

    if long_len is None:
        nlog = int(math.log2(seg))
        tabs = scan_rows(a_re, a_im, seg, nlog) + [on_rows(a_re, a_im)]
    else:
        nlog = 3
        sr, si = a_re, a_im
        for _ in range(int(math.log2(long_len // SUBLANES))):
            sr, si = csq(sr, si)
        pws = []
        qr, qi = a_re, a_im
        for _ in range(SUBLANES):
            pws.append(rows_of(qr, qi))
            qr, qi = qr * a_re - qi * a_im, qr * a_im + qi * a_re
        tabs = [on_rows(a_re, a_im)] + scan_rows(sr, si, SUBLANES, nlog) + [jnp.stack(pws, axis=1)]
    coef = jnp.stack(tabs, axis=1)
    return bblk, cblk, coef, nlog


def _s5_core(h3, tables, d_skip2, sj, s0_rows=None):
    bblk, cblk, coef, nlog = tables
    nb, rows, d = h3.shape
    nc = d // S5_CHUNK
    ns2 = bblk.shape[-1]
    ntab = coef.shape[1]
    has_s0 = s0_rows is not None
    row_block = _pick(rows, (256, 128, 64, 32, 16, 8))
    scratch = [pltpu.VMEM((rows, ns2), F32)]
    if has_s0:
        body = functools.partial(_s5_short_body, rows=rows, nlog=nlog, row_block=row_block)
    else:
        body = functools.partial(_s5_long_body, rows=rows, row_block=row_block)
        scratch.append(pltpu.VMEM((rows // SUBLANES, ns2), F32))
    in_specs = [
        pl.BlockSpec((None, rows, S5_CHUNK), lambda b, c: (b, 0, c)),
        pl.BlockSpec((None, S5_CHUNK, ns2), lambda b, c: (c, 0, 0)),
        pl.BlockSpec((None, ns2, S5_CHUNK), lambda b, c: (c, 0, 0)),
        pl.BlockSpec((None, ntab, SUBLANES, ns2), lambda b, c: (c, 0, 0, 0)),
        pl.BlockSpec((None, 1, S5_CHUNK), lambda b, c: (sj, 0, c)),
    ]
    args = [h3, bblk, cblk, coef, d_skip2]
    st_rows = rows if has_s0 else 1
    if has_s0:
        in_specs.append(pl.BlockSpec((None, None, rows, ns2), lambda b, c: (b, c, 0, 0)))
        args.append(s0_rows)
    return pl.pallas_call(
        body,
        grid=(nb, nc),
        in_specs=in_specs,
        out_specs=[
            pl.BlockSpec((None, rows, S5_CHUNK), lambda b, c: (b, 0, c)),
            pl.BlockSpec((None, None, st_rows, ns2), lambda b, c: (b, c, 0, 0)),
        ],
        out_shape=[
            jax.ShapeDtypeStruct((nb, rows, d), BF16),
            jax.ShapeDtypeStruct((nb, nc, st_rows, ns2), F32),
        ],
        scratch_shapes=scratch,
        compiler_params=_params(("parallel", "parallel")),
        name="s5_core",
    )(*args)


def _subln(o, g, out_scale):
    return o * lax.rsqrt(jnp.mean(o * o, axis=-1, keepdims=True) + LN_EPS) * g * out_scale


def _online_softmax_step(s, m, l, acc_ref, v):
    m_new = jnp.maximum(m, jnp.max(s, axis=-1, keepdims=True))
    corr = jnp.exp2(m - m_new)
    p = jnp.exp2(s - m_new)
    l_new = l * corr + jnp.sum(p, axis=-1, keepdims=True)
    acc_ref[...] = acc_ref[...] * corr + _dot(p.astype(BF16), v)
    return m_new, l_new


def _attn_prompt_body(slopes_ref, lam_ref, q_ref, k_ref, v_ref, g_ref, o_ref, bias_ref, s_ref, *, tq, nq, half,
                      out_scale):
    h = pl.program_id(1)
    neg_slope = -LOG2_E * slopes_ref[h]
    lam = lam_ref[0]
    rel = (lax.broadcasted_iota(jnp.int32, (tq, tq), 0) - lax.broadcasted_iota(jnp.int32, (tq, tq), 1)).astype(F32)
    bias_ref[0] = rel * neg_slope
    bias_ref[1] = jnp.where(rel >= 0, rel * neg_slope, NEG_INF)
    lane = lax.broadcasted_iota(jnp.int32, (tq, q_ref.shape[1]), 1)

    slot = 0
    for qi in range(nq):
        q = q_ref[qi * tq:(qi + 1) * tq, :]
        q1 = jnp.where(lane < half, q, jnp.zeros_like(q))
        q2 = jnp.where(lane >= half, q, jnp.zeros_like(q))
        offs = [neg_slope * float((qi - kj) * tq) for kj in range(qi + 1)]
        m1 = m2 = None
        for kj in range(qi + 1):
            kb = k_ref[kj * tq:(kj + 1) * tq, :]
            bias = bias_ref[1 if kj == qi else 0]
            s1 = _dot_nt(q1, kb) + bias
            s2 = _dot_nt(q2, kb) + bias
            s_ref[0, slot + kj] = s1
            s_ref[1, slot + kj] = s2
            x1 = jnp.max(s1, axis=-1, keepdims=True) + offs[kj]
            x2 = jnp.max(s2, axis=-1, keepdims=True) + offs[kj]
            m1 = x1 if m1 is None else jnp.maximum(m1, x1)
            m2 = x2 if m2 is None else jnp.maximum(m2, x2)
        acc1 = acc2 = l1 = l2 = None
        for kj in range(qi + 1):
            vb = v_ref[kj * tq:(kj + 1) * tq, :]
            p1 = jnp.exp2(s_ref[0, slot + kj] - (m1 - offs[kj]))
            p2 = jnp.exp2(s_ref[1, slot + kj] - (m2 - offs[kj]))
            d1 = _dot(p1.astype(BF16), vb)
            d2 = _dot(p2.astype(BF16), vb)
            r1 = jnp.sum(p1, axis=-1, keepdims=True)
            r2 = jnp.sum(p2, axis=-1, keepdims=True)
            acc1, acc2 = (d1, d2) if acc1 is None else (acc1 + d1, acc2 + d2)
            l1, l2 = (r1, r2) if l1 is None else (l1 + r1, l2 + r2)
        slot += qi + 1
        o = acc1 / l1 - lam * (acc2 / l2)
        o_ref[qi * tq:(qi + 1) * tq, :] = _subln(o, g_ref[...], out_scale).astype(o_ref.dtype)


def _attn_prompt(q, k, v, slopes, lam, subln_g3, aj, nb, seq, n_heads, out_scale):
    m, hd = q.shape
    dh = hd // n_heads
    tq = _pick(seq, (512, 256, 128, 64, 32, 16, 8))
    nq = seq // tq
    body = functools.partial(_attn_prompt_body, tq=tq, nq=nq, half=dh // 2, out_scale=out_scale)
    return pl.pallas_call(
        body,
        grid=(nb, n_heads),
        in_specs=[
            pl.BlockSpec(memory_space=pltpu.SMEM),
            pl.BlockSpec(memory_space=pltpu.SMEM),
            pl.BlockSpec((seq, dh), lambda b, h: (b, h)),
            pl.BlockSpec((seq, dh), lambda b, h: (b, h)),
            pl.BlockSpec((seq, dh), lambda b, h: (b, h)),
            pl.BlockSpec((None, 1, dh), lambda b, h: (aj, 0, 0)),
        ],
        out_specs=pl.BlockSpec((seq, dh), lambda b, h: (b, h)),
        out_shape=jax.ShapeDtypeStruct((m, hd), BF16),
        scratch_shapes=[pltpu.VMEM((2, tq, tq), F32), pltpu.VMEM((2, nq * (nq + 1) // 2, tq, tq), F32)],
        compiler_params=_params(("parallel", "parallel"), VMEM_LIMIT_SMALL),
        name="diff_attn_prompt",
    )(slopes, lam, q, k, v, subln_g3)


def _attn_decode_body(pt_ref, lam_ref, qc_ref, *refs, n_pages, page, gp, n_heads, t_new, out_scale):
    k_refs, v_refs = refs[:gp], refs[gp:2 * gp]
    (kn_ref, vn_ref, cmat_ref, cnew_ref, slope_ref, step_ref, g_ref, o_ref, acc_ref, m_ref, l_ref,
     s_ref) = refs[2 * gp:]
    p = pl.program_id(1)
    past = n_pages * page

    @pl.when(p == 0)
    def _():
        m_ref[...] = jnp.full_like(m_ref, NEG_INF)
        l_ref[...] = jnp.zeros_like(l_ref)
        acc_ref[...] = jnp.zeros_like(acc_ref)

    qc = qc_ref[...]
    slope = slope_ref[...]
    qpos = step_ref[...] + float(past)

    m_old = m_ref[...]
    m_new = m_old
    offs = []
    for g in range(gp):
        s = _dot_nt(qc, k_refs[g][...].astype(BF16)) + cmat_ref[...]
        s_ref[g] = s
        c = slope * (((p * gp + g) * page).astype(F32) - qpos)
        offs.append(c)
        m_new = jnp.maximum(m_new, jnp.max(s, axis=-1, keepdims=True) + c)
    lsum = jnp.zeros_like(m_new)
    pv = jnp.zeros(acc_ref.shape, F32)
    for g in range(gp):
        pe = jnp.exp2(s_ref[g] - (m_new - offs[g]))
        lsum = lsum + jnp.sum(pe, axis=-1, keepdims=True)
        pv = pv + _dot(pe.astype(BF16), v_refs[g][...].astype(BF16))
    corr = jnp.exp2(m_old - m_new)
    m_ref[...] = m_new
    l_ref[...] = l_ref[...] * corr + lsum
    acc_ref[...] = acc_ref[...] * corr + pv

    @pl.when(p == n_pages // gp - 1)
    def _():
        s = _dot_nt(qc, kn_ref[...]) + cnew_ref[...]
        m_fin, l_fin = _online_softmax_step(s, m_ref[...], l_ref[...], acc_ref, vn_ref[...])
        lam = lam_ref[0]
        inv = 1.0 / l_fin
        rph = acc_ref.shape[0] // n_heads
        dh = acc_ref.shape[1]
        for h in range(n_heads):
            blk = acc_ref[h * rph:(h + 1) * rph, :] * inv[h * rph:(h + 1) * rph, :]
            o = blk[0:t_new, :] - lam * blk[rph // 2:rph // 2 + t_new, :]
            o_ref[:, h * dh:(h + 1) * dh] = _subln(o, g_ref[...], out_scale)


def _attn_decode(q, k_new, v_new, cache_k4, cache_v4, page_table, slopes, lam, subln_g3, aj, nb, t_new, n_heads,
                 out_scale):
    hd = q.shape[1]
    dh = hd // n_heads
    half = dh // 2
    prows = cache_k4.shape[2]
    page = prows // n_heads
    n_pages = page_table.shape[1]
    gp = _pick(n_pages, (8, 4, 2, 1))
    rph = SUBLANES
    tslots = SUBLANES
    assert t_new <= rph // 2 and t_new <= tslots
    nr = n_heads * rph
    q4 = jnp.transpose(q.reshape(nb, t_new, n_heads, dh), (0, 2, 1, 3))
    q4 = jnp.pad(q4, ((0, 0), (0, 0), (0, rph // 2 - t_new), (0, 0)))
    lane = jnp.arange(dh)
    qc = jnp.stack([jnp.where(lane < half, q4, 0), jnp.where(lane >= half, q4, 0)], axis=2).reshape(nb, nr, dh)
    pad_tok = ((0, 0), (0, tslots - t_new), (0, 0), (0, 0))
    kn = jnp.pad(k_new.reshape(nb, t_new, n_heads, dh), pad_tok).reshape(nb, tslots * n_heads, dh)
    vn = jnp.pad(v_new.reshape(nb, t_new, n_heads, dh), pad_tok).reshape(nb, tslots * n_heads, dh)
    row = jnp.arange(nr)
    row_head, row_step = row // rph, (row % (rph // 2)).astype(F32)
    slope_rows = LOG2_E * slopes[row_head][:, None]
    step_rows = row_step[:, None]
    col = jnp.arange(prows)
    cmat = jnp.where((col % n_heads)[None, :] == row_head[:, None],
                     slope_rows * (col // n_heads).astype(F32)[None, :], NEG_INF)
    coln = jnp.arange(tslots * n_heads)
    tokn = (coln // n_heads).astype(F32)[None, :]
    ok = jnp.logical_and((coln % n_heads)[None, :] == row_head[:, None],
                         jnp.logical_and(tokn < t_new, tokn <= step_rows))
    cnew = jnp.where(ok, -slope_rows * (step_rows - tokn), NEG_INF)
    body = functools.partial(_attn_decode_body, n_pages=n_pages, page=page, gp=gp, n_heads=n_heads, t_new=t_new,
                             out_scale=out_scale)

    def page_spec(g):
        return pl.BlockSpec((None, None, prows, dh), lambda b, p, pt: (aj, pt[b * n_pages + p * gp + g], 0, 0))

    grid_spec = pltpu.PrefetchScalarGridSpec(
        num_scalar_prefetch=1,
        grid=(nb, n_pages // gp),
        in_specs=[
            pl.BlockSpec(memory_space=pltpu.SMEM),
            pl.BlockSpec((None, nr, dh), lambda b, p, pt: (b, 0, 0)),
            *[page_spec(g) for g in range(gp)],
            *[page_spec(g) for g in range(gp)],
            pl.BlockSpec((None, tslots * n_heads, dh), lambda b, p, pt: (b, 0, 0)),
            pl.BlockSpec((None, tslots * n_heads, dh), lambda b, p, pt: (b, 0, 0)),
            pl.BlockSpec((nr, prows), lambda b, p, pt: (0, 0)),
            pl.BlockSpec((nr, tslots * n_heads), lambda b, p, pt: (0, 0)),
            pl.BlockSpec((nr, 1), lambda b, p, pt: (0, 0)),
            pl.BlockSpec((nr, 1), lambda b, p, pt: (0, 0)),
            pl.BlockSpec((None, 1, dh), lambda b, p, pt: (aj, 0, 0)),
        ],
        out_specs=pl.BlockSpec((None, t_new, hd), lambda b, p, pt: (b, 0, 0)),
        scratch_shapes=[pltpu.VMEM((nr, dh), F32), pltpu.VMEM((nr, 1), F32), pltpu.VMEM((nr, 1), F32),
                        pltpu.VMEM((gp, nr, prows), F32)],
    )
    return pl.pallas_call(
        body,
        grid_spec=grid_spec,
        out_shape=jax.ShapeDtypeStruct((nb, t_new, hd), F32),
        compiler_params=_params(("parallel", "arbitrary")),
        name="diff_attn_decode",
    )(page_table.reshape(-1), lam, qc, *([cache_k4] * gp), *([cache_v4] * gp), kn, vn, cmat, cnew, slope_rows,
      step_rows, subln_g3)


def kernel(x_prompt, x_sample, cache_k, cache_v, state_s5_re, state_s5_im, page_table, p_prompt, p_sample,
           ln_g, ln_b, w_ffn_gate, w_ffn_up, w_ffn_down, w_pe, w_pe_gate,
           s5_lambda_re, s5_lambda_im, s5_log_dt, s5_b_re, s5_b_im, s5_c_re, s5_c_im, s5_d,
           s5_glu_w, s5_glu_b, s5_w_out,
           attn_w_qkv, attn_lambda_q1, attn_lambda_k1, attn_lambda_q2, attn_lambda_k2, attn_subln_g, attn_w_o):
    depth = ln_g.shape[0]
    alpha = (2.0 * depth) ** 0.25
    nbp, seq, d = x_prompt.shape
    nbs, t_new, _ = x_sample.shape
    n_attn, n_pool, page, n_heads, qk_dim = cache_k.shape
    v_dim = cache_v.shape[-1]
    half = qk_dim // 2
    attn_scale = 1.0 / math.sqrt(half)
    g_cnt, p_cnt = s5_lambda_re.shape[1:]
    gw = s5_b_re.shape[-1]
    gpc = S5_CHUNK // gw
    nc = d // S5_CHUNK
    hq = n_heads * qk_dim

    ln_g4 = ln_g.reshape(depth, ln_g.shape[1], 1, d)
    ln_b4 = ln_b.reshape(depth, ln_b.shape[1], 1, d)
    pp3 = p_prompt.reshape(depth, nbp * seq, -1)
    ps3 = p_sample.reshape(depth, nbs * t_new, -1)
    cache_k4 = cache_k.reshape(n_attn, n_pool, page * n_heads, qk_dim)
    cache_v4 = cache_v.reshape(n_attn, n_pool, page * n_heads, v_dim)
    attn_w_qkv = attn_w_qkv.astype(BF16)
    attn_w_o = attn_w_o.astype(BF16)
    s5_glu_w = s5_glu_w.astype(BF16)
    s5_w_out = s5_w_out.astype(BF16)
    w_pe_gate = w_pe_gate.astype(BF16)
    w_pe = w_pe.astype(BF16)
    glu_b3 = s5_glu_b.reshape(s5_glu_b.shape[0], 1, d)
    d_skip3 = s5_d.reshape(s5_d.shape[0], 1, d)
    subln_g3 = attn_subln_g.reshape(n_attn, 1, v_dim)
    slopes = jnp.exp2(-8.0 * jnp.arange(1, n_heads + 1, dtype=F32) / n_heads)

    hp = x_prompt.reshape(nbp * seq, d)
    hs = x_sample.reshape(nbs * t_new, d)
    kp, vp, ksm, vsm = [], [], [], []
    srp, sip, srs, sis = [], [], [], []

    def split_state(st):
        lead = st.shape[:-2]
        st = st.reshape(lead + (nc, 2, gpc, p_cnt))
        return (st[..., 0, :, :].reshape(lead + (g_cnt, p_cnt)), st[..., 1, :, :].reshape(lead + (g_cnt, p_cnt)))

    for i in range(depth):
        j = i // 2
        hp, hs = _ffn_ln(hp, hs, w_ffn_gate, w_ffn_up, w_ffn_down, ln_g4, ln_b4, i, 0, 0, alpha)
        if i % 2 == 0:
            prm = (s5_lambda_re[j], s5_lambda_im[j], s5_log_dt[j], s5_b_re[j], s5_b_im[j], s5_c_re[j], s5_c_im[j])
            tabs_p = _s5_tables(*prm, long_len=seq)
            seg_len = seq // SUBLANES
            hperm = jnp.transpose(hp.reshape(nbp, SUBLANES, seg_len, d), (0, 2, 1, 3)).reshape(nbp, seq, d)
            yperm, stp = _s5_core(hperm, tabs_p, d_skip3, j)
            yp = jnp.transpose(yperm.reshape(nbp, seg_len, SUBLANES, d), (0, 2, 1, 3)).reshape(nbp * seq, d)
            re, im = split_state(stp[:, :, 0, :])
            srp.append(re)
            sip.append(im)
            hp = _glu_out_ln(yp, hp, s5_glu_w, glu_b3, s5_w_out, ln_g4, ln_b4, j, i, 1, alpha)
            tabs_s = _s5_tables(*prm, seg=t_new)
            s0 = jnp.concatenate([state_s5_re[j].reshape(nbs, nc, gpc * p_cnt),
                                  state_s5_im[j].reshape(nbs, nc, gpc * p_cnt)], axis=-1)
            s0_rows = jnp.zeros((nbs, t_new, nc, s0.shape[-1]), F32).at[:, 0].set(s0)
            s0_rows = jnp.transpose(s0_rows.reshape(nbs * t_new, nc, -1), (1, 0, 2))[None]
            ys, sts = _s5_core(hs.reshape(1, nbs * t_new, d), tabs_s, d_skip3, j, s0_rows)
            st_last = sts[0].reshape(nc, nbs, t_new, -1)[:, :, t_new - 1, :]
            re, im = split_state(jnp.transpose(st_last, (1, 0, 2)))
            srs.append(re)
            sis.append(im)
            hs = _glu_out_ln(ys.reshape(nbs * t_new, d), hs, s5_glu_w, glu_b3, s5_w_out, ln_g4, ln_b4, j, i, 1,
                             alpha)
        else:
            lam_init = 0.8 - 0.6 * math.exp(-0.3 * i)
            lam = (jnp.exp(jnp.sum(attn_lambda_q1[j] * attn_lambda_k1[j]))
                   - jnp.exp(jnp.sum(attn_lambda_q2[j] * attn_lambda_k2[j])) + lam_init).reshape(1).astype(F32)
            out_scale = 1.0 - lam_init
            qp, kf, kb, vf, vb = _qkv_proj(hp, attn_w_qkv, j, hq, hq, n_heads * v_dim, attn_scale * LOG2_E)
            kp.append(kf.reshape(nbp, seq, n_heads, qk_dim))
            vp.append(vf.reshape(nbp, seq, n_heads, v_dim))
            op = _attn_prompt(qp, kb, vb, slopes, lam, subln_g3, j, nbp, seq, n_heads, out_scale)
            hp = _proj_ln(op, hp, attn_w_o, j, ln_g4, ln_b4, i, 1, alpha)
            qs, kf, kb, vf, vb = _qkv_proj(hs, attn_w_qkv, j, hq, hq, n_heads * v_dim, attn_scale * LOG2_E)
            ksm.append(kf.reshape(nbs, t_new, n_heads, qk_dim))
            vsm.append(vf.reshape(nbs, t_new, n_heads, v_dim))
            osm = _attn_decode(qs, kb, vb, cache_k4, cache_v4, page_table, slopes, lam, subln_g3, j, nbs, t_new,
                               n_heads, out_scale)
            hs = _proj_ln(osm.reshape(nbs * t_new, n_heads * v_dim), hs, attn_w_o, j, ln_g4, ln_b4, i, 1, alpha)
        hp, hs = _ffn_ln(hp, hs, w_ffn_gate, w_ffn_up, w_ffn_down, ln_g4, ln_b4, i, 1, 2, alpha)
        hp = _pe_gate(hp, pp3, w_pe, w_pe_gate, i)
        hs = _pe_gate(hs, ps3, w_pe, w_pe_gate, i)

    return (hp.reshape(nbp, seq, d), hs.reshape(nbs, t_new, d),
            jnp.stack(kp), jnp.stack(vp), jnp.stack(ksm), jnp.stack(vsm),
            jnp.stack(srp), jnp.stack(sip), jnp.stack(srs), jnp.stack(sis))
```

```python
import functools
import math

import jax
import jax.numpy as jnp
from jax import lax
from jax.experimental import pallas as pl
from jax.experimental.pallas import tpu as pltpu

F32 = jnp.float32
BF16 = jnp.bfloat16
LN_EPS = 1e-5
NEG_INF = -1e30
LOG2_E = math.log2(math.e)
LANES = 128
S5_CHUNK = 2 * LANES
SUBLANES = 8
VMEM_LIMIT_MAX = 60 * 1024 * 1024
VMEM_LIMIT_BIG = 56 * 1024 * 1024
VMEM_LIMIT_SMALL = 40 * 1024 * 1024
EPILOGUE_ROWS = 128
ROW_CHUNK = 128


def _pick(n, cands):
    for c in cands:
        if n % c == 0:
            return c
    return n


def _row_tile(m):
    return _pick(m, (1024, 512, 256, 128, 64, 32, 16, 8))


def _params(sem, vmem=VMEM_LIMIT_BIG):
    return pltpu.CompilerParams(dimension_semantics=sem, vmem_limit_bytes=vmem)


def _dot(a, b):
    return jnp.dot(a, b, preferred_element_type=F32)


def _dot_nt(a, b):
    return lax.dot_general(a, b, (((1,), (1,)), ((), ())), preferred_element_type=F32)


def _layer_norm(z, g, b):
    mu = jnp.mean(z, axis=-1, keepdims=True)
    zc = z - mu
    var = jnp.mean(zc * zc, axis=-1, keepdims=True)
    return zc * lax.rsqrt(var + LN_EPS) * g + b


def _ln_epilogue(o_ref, res_ref, g_ref, b_ref, alpha, scale):
    tm = o_ref.shape[0]
    rb = min(EPILOGUE_ROWS, tm)
    g = g_ref[...]
    b = b_ref[...]

    def chunk(r, carry):
        rows = pl.ds(pl.multiple_of(r * rb, rb), rb)
        z = o_ref[rows, :] if scale == 1.0 else scale * o_ref[rows, :]
        if res_ref is not None:
            z = z + alpha * res_ref[rows, :].astype(F32)
        o_ref[rows, :] = _layer_norm(z, g, b)
        return carry

    lax.fori_loop(0, tm // rb, chunk, 0)


def _ffn_ln_body(x_hbm, xs_ref, wg_ref, wu_ref, wd_ref, g_ref, b_ref, o_ref, os_ref, xin_ref, xb_ref, sem, *,
                 alpha, nf, tm, ms, col_chunk):
    i = pl.program_id(0)
    j = pl.program_id(1)
    n_tiles = pl.num_programs(0)
    d = o_ref.shape[1]
    rb = min(EPILOGUE_ROWS, tm)

    def x_copy(tile):
        return pltpu.make_async_copy(x_hbm.at[pl.ds(tile * tm, tm), :], xin_ref, sem)

    @pl.when(j == 0)
    def _():
        @pl.when(i == 0)
        def _():
            x_copy(0).start()
            xs = xs_ref[...]
            xb_ref[tm:tm + ms, :] = xs.astype(BF16)
            os_ref[...] = (2.0 * alpha) * xs

        x_copy(i).wait()

        def chunk(r, carry):
            rows = pl.ds(pl.multiple_of(r * rb, rb), rb)
            xr = xin_ref[rows, :]
            xb_ref[rows, :] = xr.astype(BF16)
            o_ref[rows, :] = (2.0 * alpha) * xr
            return carry

        lax.fori_loop(0, tm // rb, chunk, 0)

    @pl.when(jnp.logical_and(j == 1, i + 1 < n_tiles))
    def _():
        x_copy(i + 1).start()

    def hidden(nrows):
        xb = xb_ref[0:nrows, :]
        gate = _dot(xb, wg_ref[...].astype(BF16))
        up = _dot(xb, wu_ref[...].astype(BF16))
        return (gate * jax.nn.sigmoid(gate) * up).astype(BF16)

    def step(nrows):
        h = hidden(nrows)
        for c in range(d // col_chunk):
            cols = slice(c * col_chunk, (c + 1) * col_chunk)
            r = _dot(h, wd_ref[:, cols].astype(BF16))
            o_ref[:, cols] += r[0:tm, :]
            if nrows > tm:
                os_ref[:, cols] += r[tm:nrows, :]

    def last_step(nrows):
        h = hidden(nrows)
        wd = wd_ref[...].astype(BF16)
        g, b = g_ref[...], b_ref[...]
        for rows in _row_chunks(tm):
            z = 0.5 * (o_ref[rows, :] + _dot(h[rows, :], wd))
            o_ref[rows, :] = _layer_norm(z, g, b)
        if nrows > tm:
            z = 0.5 * (os_ref[...] + _dot(h[tm:nrows, :], wd))
            os_ref[...] = _layer_norm(z, g, b)

    last = j == nf - 1
    first_tile = i == 0
    for is_last, is_first, fn, nrows in ((False, True, step, tm + ms), (False, False, step, tm),
                                         (True, True, last_step, tm + ms), (True, False, last_step, tm)):
        cond = jnp.logical_and(last if is_last else jnp.logical_not(last),
                               first_tile if is_first else jnp.logical_not(first_tile))
        pl.when(cond)(functools.partial(fn, nrows))


def _ffn_ln(x, xs, w_gate, w_up, w_down, ln_g4, ln_b4, li, si, ln_idx, alpha):
    m, d = x.shape
    ms = xs.shape[0]
    f = w_gate.shape[-1]
    tm = _row_tile(m)
    tf = _pick(f, (512, 256, 128))
    nf = f // tf
    assert nf >= 2 and ms % 16 == 0
    body = functools.partial(_ffn_ln_body, alpha=alpha, nf=nf, tm=tm, ms=ms, col_chunk=_pick(d, (512, 256, 128)))
    return pl.pallas_call(
        body,
        grid=(m // tm, nf),
        in_specs=[
            pl.BlockSpec(memory_space=pl.ANY),
            pl.BlockSpec((ms, d), lambda i, j: (0, 0)),
            pl.BlockSpec((None, None, d, tf), lambda i, j: (li, si, 0, j)),
            pl.BlockSpec((None, None, d, tf), lambda i, j: (li, si, 0, j)),
            pl.BlockSpec((None, None, tf, d), lambda i, j: (li, si, j, 0)),
            pl.BlockSpec((None, None, 1, d), lambda i, j: (li, ln_idx, 0, 0)),
            pl.BlockSpec((None, None, 1, d), lambda i, j: (li, ln_idx, 0, 0)),
        ],
        out_specs=[pl.BlockSpec((tm, d), lambda i, j: (i, 0)), pl.BlockSpec((ms, d), lambda i, j: (0, 0))],
        out_shape=[jax.ShapeDtypeStruct((m, d), F32), jax.ShapeDtypeStruct((ms, d), F32)],
        scratch_shapes=[pltpu.VMEM((tm, d), F32), pltpu.VMEM((tm + ms, d), BF16), pltpu.SemaphoreType.DMA(())],
        compiler_params=_params(("arbitrary", "arbitrary"), VMEM_LIMIT_MAX),
        name="ffn_ln",
    )(x, xs, w_gate, w_up, w_down, ln_g4, ln_b4)


def _resident(block_shape, index_map):
    return pl.BlockSpec(block_shape, index_map, pipeline_mode=pl.Buffered(1))


def _row_chunks(tm):
    rc = min(ROW_CHUNK, tm)
    return [slice(r * rc, (r + 1) * rc) for r in range(tm // rc)]


def _proj_tile(m):
    return _pick(m, (512, 256, 128, 64, 32, 16, 8))


def _ln_specs(li, ln_idx, d):
    spec = _resident((None, None, 1, d), lambda i: (li, ln_idx, 0, 0))
    return [spec, spec]


def _glu_out_ln_body(y_ref, res_ref, wglu_ref, bglu_ref, wout_ref, g_ref, b_ref, o_ref, *, alpha):
    for rows in _row_chunks(o_ref.shape[0]):
        y = y_ref[rows, :]
        t = _dot(y, wglu_ref[...]) + bglu_ref[...]
        z = (y.astype(F32) * jax.nn.sigmoid(t)).astype(BF16)
        o = _dot(z, wout_ref[...])
        o_ref[rows, :] = _layer_norm(alpha * res_ref[rows, :] + o, g_ref[...], b_ref[...])


def _glu_out_ln(y, res, glu_w, glu_b3, w_out, ln_g4, ln_b4, sj, li, ln_idx, alpha):
    m, d = res.shape
    tm = _proj_tile(m)
    return pl.pallas_call(
        functools.partial(_glu_out_ln_body, alpha=alpha),
        grid=(m // tm,),
        in_specs=[
            pl.BlockSpec((tm, d), lambda i: (i, 0)),
            pl.BlockSpec((tm, d), lambda i: (i, 0)),
            _resident((None, d, d), lambda i: (sj, 0, 0)),
            _resident((None, 1, d), lambda i: (sj, 0, 0)),
            _resident((None, d, d), lambda i: (sj, 0, 0)),
            *_ln_specs(li, ln_idx, d),
        ],
        out_specs=pl.BlockSpec((tm, d), lambda i: (i, 0)),
        out_shape=jax.ShapeDtypeStruct((m, d), F32),
        compiler_params=_params(("parallel",)),
        name="s5_glu_out_ln",
    )(y, res, glu_w, glu_b3, w_out, ln_g4, ln_b4)


def _qkv_body(x_ref, w_ref, *refs, scale, n_q, n_k, tn, slab):
    q_ref, kf_ref, kb_ref, vf_ref, vb_ref = refs[-5:]
    xb = x_ref[...].astype(BF16)

    def put(f_ref, b_ref, lo, r):
        cols = slice(lo, lo + tn)
        b_ref[:, cols] = r.astype(b_ref.dtype)
        if slab is None:
            f_ref[:, cols] = r
        else:
            for s in range(f_ref.shape[0]):
                f_ref[s, :, cols] = r if s == slab else jnp.zeros_like(r)

    for c in range(w_ref.shape[1] // tn):
        lo = c * tn
        r = _dot(xb, w_ref[:, lo:lo + tn])
        if lo < n_q:
            q_ref[:, lo:lo + tn] = (r * scale).astype(q_ref.dtype)
        elif lo < n_q + n_k:
            put(kf_ref, kb_ref, lo - n_q, r)
        else:
            put(vf_ref, vb_ref, lo - n_q - n_k, r)


def _qkv_proj(x, w3, wj, n_q, n_k, n_v, scale, n_slabs=None, stacked=None):
    m, d = x.shape
    tm = _pick(m, (256, 128, 64, 32, 16, 8))
    tn = _pick(math.gcd(n_q, math.gcd(n_k, n_v)), (512, 256, 128))
    in_specs = [
        pl.BlockSpec((tm, d), lambda i: (i, 0)),
        _resident((None, d, n_q + n_k + n_v), lambda i: (wj, 0, 0)),
    ]
    args = [x, w3]

    def row_spec(n):
        return pl.BlockSpec((tm, n), lambda i: (i, 0))

    def f32_out(n):
        if n_slabs is None:
            return row_spec(n), jax.ShapeDtypeStruct((m, n), F32)
        if stacked is None:
            return pl.BlockSpec((n_slabs, tm, n), lambda i: (0, i, 0)), jax.ShapeDtypeStruct((n_slabs, m, n), F32)
        return pl.BlockSpec((None, tm, n), lambda i: (wj, i, 0)), jax.ShapeDtypeStruct((n_slabs, m, n), F32)

    aliases = {}
    if stacked is not None:
        in_specs += [pl.BlockSpec(memory_space=pl.ANY), pl.BlockSpec(memory_space=pl.ANY)]
        args += list(stacked)
        aliases = {2: 1, 3: 3}
    (k_spec, k_shape), (v_spec, v_shape) = f32_out(n_k), f32_out(n_v)
    body = functools.partial(_qkv_body, scale=scale, n_q=n_q, n_k=n_k, tn=tn,
                             slab=wj if (n_slabs is not None and stacked is None) else None)
    return pl.pallas_call(
        body,
        grid=(m // tm,),
        in_specs=in_specs,
        out_specs=[row_spec(n_q), k_spec, row_spec(n_k), v_spec, row_spec(n_v)],
        out_shape=[jax.ShapeDtypeStruct((m, n_q), BF16), k_shape, jax.ShapeDtypeStruct((m, n_k), BF16), v_shape,
                   jax.ShapeDtypeStruct((m, n_v), BF16)],
        input_output_aliases=aliases,
        compiler_params=_params(("parallel",)),
        name="qkv_proj",
    )(*args)


def _proj_ln_body(x_ref, res_ref, w_ref, g_ref, b_ref, o_ref, *, alpha):
    for rows in _row_chunks(o_ref.shape[0]):
        o = _dot(x_ref[rows, :].astype(BF16), w_ref[...])
        o_ref[rows, :] = _layer_norm(alpha * res_ref[rows, :] + o, g_ref[...], b_ref[...])


def _proj_ln(x, res, w3, wj, ln_g4, ln_b4, li, ln_idx, alpha):
    m, k = x.shape
    d = res.shape[1]
    tm = _proj_tile(m)
    return pl.pallas_call(
        functools.partial(_proj_ln_body, alpha=alpha),
        grid=(m // tm,),
        in_specs=[
            pl.BlockSpec((tm, k), lambda i: (i, 0)),
            pl.BlockSpec((tm, d), lambda i: (i, 0)),
            _resident((None, k, d), lambda i: (wj, 0, 0)),
            *_ln_specs(li, ln_idx, d),
        ],
        out_specs=pl.BlockSpec((tm, d), lambda i: (i, 0)),
        out_shape=jax.ShapeDtypeStruct((m, d), F32),
        compiler_params=_params(("parallel",)),
        name="proj_ln",
    )(x, res, w3, ln_g4, ln_b4)


def _pe_body(h_ref, p_ref, wg_ref, wp_ref, o_ref):
    for rows in _row_chunks(o_ref.shape[0]):
        h = h_ref[rows, :]
        gate = _dot(h.astype(BF16), wg_ref[...])
        pe = _dot(p_ref[rows, :].astype(BF16), wp_ref[...])
        o_ref[rows, :] = h + jax.nn.sigmoid(gate) * pe


def _pe_gate(h, p3, w_pe, w_pe_gate, li):
    m, d = h.shape
    pd = p3.shape[-1]
    tm = _proj_tile(m)
    return pl.pallas_call(
        _pe_body,
        grid=(m // tm,),
        in_specs=[
            pl.BlockSpec((tm, d), lambda i: (i, 0)),
            pl.BlockSpec((None, tm, pd), lambda i: (li, i, 0)),
            _resident((None, d, d), lambda i: (li, 0, 0)),
            _resident((None, pd, d), lambda i: (li, 0, 0)),
        ],
        out_specs=pl.BlockSpec((tm, d), lambda i: (i, 0)),
        out_shape=jax.ShapeDtypeStruct((m, d), F32),
        compiler_params=_params(("parallel",)),
        name="pe_gate",
    )(h, p3, w_pe_gate, w_pe)


def _cmul_add(sr, si, ar, ai, pr, pi):
    return sr + ar * pr - ai * pi, si + ar * pi + ai * pr


def _sublane_scan(sr, si, coef_ref, first, nlog, re, im):
    for k in range(nlog):
        ar, ai = coef_ref[first + k, :, re], coef_ref[first + k, :, im]
        pr = pltpu.roll(sr, 1 << k, axis=0)
        pi = pltpu.roll(si, 1 << k, axis=0)
        sr, si = _cmul_add(sr, si, ar, ai, pr, pi)
    return sr, si


def _s5_in_proj(x_ref, bblk_ref, bu_ref, rows, row_block):
    for rb in range(rows // row_block):
        r = slice(rb * row_block, (rb + 1) * row_block)
        bu_ref[r, :] = _dot(x_ref[r, :].astype(BF16), bblk_ref[...])


def _s5_out_proj(x_ref, s_ref, cblk_ref, d_ref, y_ref, rows, row_block):
    dvec = d_ref[...]
    for rb in range(rows // row_block):
        r = slice(rb * row_block, (rb + 1) * row_block)
        y = _dot(s_ref[r, :].astype(BF16), cblk_ref[...]) + dvec * x_ref[r, :]
        y_ref[r, :] = jax.nn.gelu(y).astype(y_ref.dtype)


def _s5_short_body(x_ref, bblk_ref, cblk_ref, coef_ref, d_ref, s0_ref, y_ref, st_ref, bu_ref, *, rows, nlog,
                   row_block):
    ns = bu_ref.shape[1] // 2
    re, im = slice(0, ns), slice(ns, 2 * ns)
    _s5_in_proj(x_ref, bblk_ref, bu_ref, rows, row_block)
    ar, ai = coef_ref[nlog, 0:1, re], coef_ref[nlog, 0:1, im]
    br, bi = _cmul_add(bu_ref[:, re], bu_ref[:, im], ar, ai, s0_ref[:, re], s0_ref[:, im])
    bu_ref[:, re] = br
    bu_ref[:, im] = bi

    def group(gi, c):
        rws = pl.ds(pl.multiple_of(gi * SUBLANES, SUBLANES), SUBLANES)
        sr, si = _sublane_scan(bu_ref[rws, re], bu_ref[rws, im], coef_ref, 0, nlog, re, im)
        bu_ref[rws, re] = sr
        bu_ref[rws, im] = si
        return c

    lax.fori_loop(0, rows // SUBLANES, group, 0)
    st_ref[...] = bu_ref[...]
    _s5_out_proj(x_ref, bu_ref, cblk_ref, d_ref, y_ref, rows, row_block)


def _s5_long_body(x_ref, bblk_ref, cblk_ref, coef_ref, d_ref, y_ref, st_ref, bu_ref, pw_ref, *, rows, row_block):
    ns = bu_ref.shape[1] // 2
    re, im = slice(0, ns), slice(ns, 2 * ns)
    seg_len = rows // SUBLANES
    _s5_in_proj(x_ref, bblk_ref, bu_ref, rows, row_block)

    ar, ai = coef_ref[0, :, re], coef_ref[0, :, im]

    def step(i, c):
        rws = pl.ds(pl.multiple_of(i * SUBLANES, SUBLANES), SUBLANES)
        sr, si = _cmul_add(bu_ref[rws, re], bu_ref[rws, im], ar, ai, c[0], c[1])
        bu_ref[rws, re] = sr
        bu_ref[rws, im] = si
        return sr, si

    zero = jnp.zeros((SUBLANES, ns), F32)
    er, ei = lax.fori_loop(0, seg_len, step, (zero, zero), unroll=4)

    er, ei = _sublane_scan(er, ei, coef_ref, 1, 3, re, im)
    st_ref[:, re] = er[SUBLANES - 1:SUBLANES, :]
    st_ref[:, im] = ei[SUBLANES - 1:SUBLANES, :]
    first = lax.broadcasted_iota(jnp.int32, (SUBLANES, ns), 0) == 0
    br = jnp.where(first, 0.0, pltpu.roll(er, 1, axis=0))
    bi = jnp.where(first, 0.0, pltpu.roll(ei, 1, axis=0))

    pw_ref[0:SUBLANES, :] = coef_ref[4]
    n = SUBLANES
    while n < seg_len:
        lr, li = pw_ref[n - 1:n, re], pw_ref[n - 1:n, im]
        qr, qi = pw_ref[0:n, re], pw_ref[0:n, im]
        pw_ref[n:2 * n, re] = qr * lr - qi * li
        pw_ref[n:2 * n, im] = qr * li + qi * lr
        n *= 2

    def fix(i, c):
        rws = pl.ds(pl.multiple_of(i * SUBLANES, SUBLANES), SUBLANES)
        sr, si = _cmul_add(bu_ref[rws, re], bu_ref[rws, im], pw_ref[pl.ds(i, 1), re], pw_ref[pl.ds(i, 1), im],
                           br, bi)
        bu_ref[rws, re] = sr
        bu_ref[rws, im] = si
        return c

    lax.fori_loop(0, seg_len, fix, 0, unroll=4)
    _s5_out_proj(x_ref, bu_ref, cblk_ref, d_ref, y_ref, rows, row_block)


def _s5_tables(lam_re, lam_im, log_dt, b_re, b_im, c_re, c_im, seg=None, long_len=None):
    g, p = lam_re.shape
    w = b_re.shape[-1]
    gpc = S5_CHUNK // w
    nc = g // gpc
    dt = jnp.exp(log_dt)[:, None]
    mag = jnp.exp(lam_re * dt)
    ang = lam_im * dt
    a_re, a_im = mag * jnp.cos(ang), mag * jnp.sin(ang)
    den = lam_re * lam_re + lam_im * lam_im
    coef_re = ((a_re - 1.0) * lam_re + a_im * lam_im) / den
    coef_im = (a_im * lam_re - (a_re - 1.0) * lam_im) / den
    bb_re = coef_re[..., None] * b_re - coef_im[..., None] * b_im
    bb_im = coef_re[..., None] * b_im + coef_im[..., None] * b_re
    def blockdiag(t):
        a, b = t.shape[1:]
        tiled = jnp.tile(t.reshape(nc, gpc * a, b), (1, 1, gpc))
        same = (jnp.arange(gpc * a) // a)[:, None] == (jnp.arange(gpc * b) // b)[None, :]
        return jnp.where(same[None], tiled, 0.0)

    def blockdiag_in(bb):
        return blockdiag(jnp.transpose(bb, (0, 2, 1)))

    def blockdiag_out(cc):
        return blockdiag(jnp.transpose(cc, (0, 2, 1)))

    bblk = jnp.concatenate([blockdiag_in(bb_re), blockdiag_in(bb_im)], axis=-1).astype(BF16)
    cblk = jnp.concatenate([blockdiag_out(c_re), -blockdiag_out(c_im)], axis=1).astype(BF16)

    def rows_of(zr, zi):
        return jnp.concatenate([zr.reshape(nc, gpc * p), zi.reshape(nc, gpc * p)], axis=-1)

    def csq(zr, zi):
        return zr * zr - zi * zi, 2.0 * zr * zi

    def on_rows(zr, zi):
        z = rows_of(zr, zi)
        return jnp.broadcast_to(z[:, None, :], z.shape[:1] + (SUBLANES,) + z.shape[1:])

    def scan_rows(zr, zi, seg, nlog):
        row = jnp.arange(SUBLANES)
        out = []
        for k in range(nlog):
            keep = ((row % seg) >= (1 << k)).astype(F32)[None, :, None]
            out.append(on_rows(zr, zi) * keep)
            zr, zi = csq(zr, zi)
        return out

    if long_len is None:
        nlog = int(math.log2(seg))
        tabs = scan_rows(a_re, a_im, seg, nlog) + [on_rows(a_re, a_im)]
    else:
        nlog = 3
        sr, si = a_re, a_im
        for _ in range(int(math.log2(long_len // SUBLANES))):
            sr, si = csq(sr, si)
        pws = []
        qr, qi = a_re, a_im
        for _ in range(SUBLANES):
            pws.append(rows_of(qr, qi))
            qr, qi = qr * a_re - qi * a_im, qr * a_im + qi * a_re
        tabs = [on_rows(a_re, a_im)] + scan_rows(sr, si, SUBLANES, nlog) + [jnp.stack(pws, axis=1)]
    coef = jnp.stack(tabs, axis=1)
    return bblk, cblk, coef, nlog


def _s5_core(h3, tables, d_skip2, sj, s0_rows=None):
    bblk, cblk, coef, nlog = tables
    nb, rows, d = h3.shape
    nc = d // S5_CHUNK
    ns2 = bblk.shape[-1]
    ntab = coef.shape[1]
    has_s0 = s0_rows is not None
    row_block = _pick(rows, (256, 128, 64, 32, 16, 8))
    scratch = [pltpu.VMEM((rows, ns2), F32)]
    if has_s0:
        body = functools.partial(_s5_short_body, rows=rows, nlog=nlog, row_block=row_block)
    else:
        body = functools.partial(_s5_long_body, rows=rows, row_block=row_block)
        scratch.append(pltpu.VMEM((rows // SUBLANES, ns2), F32))
    in_specs = [
        pl.BlockSpec((None, rows, S5_CHUNK), lambda b, c: (b, 0, c)),
        pl.BlockSpec((None, S5_CHUNK, ns2), lambda b, c: (c, 0, 0)),
        pl.BlockSpec((None, ns2, S5_CHUNK), lambda b, c: (c, 0, 0)),
        pl.BlockSpec((None, ntab, SUBLANES, ns2), lambda b, c: (c, 0, 0, 0)),
        pl.BlockSpec((None, 1, S5_CHUNK), lambda b, c: (sj, 0, c)),
    ]
    args = [h3, bblk, cblk, coef, d_skip2]
    st_rows = rows if has_s0 else 1
    if has_s0:
        in_specs.append(pl.BlockSpec((None, None, rows, ns2), lambda b, c: (b, c, 0, 0)))
        args.append(s0_rows)
    return pl.pallas_call(
        body,
        grid=(nb, nc),
        in_specs=in_specs,
        out_specs=[
            pl.BlockSpec((None, rows, S5_CHUNK), lambda b, c: (b, 0, c)),
            pl.BlockSpec((None, None, st_rows, ns2), lambda b, c: (b, c, 0, 0)),
        ],
        out_shape=[
            jax.ShapeDtypeStruct((nb, rows, d), BF16),
            jax.ShapeDtypeStruct((nb, nc, st_rows, ns2), F32),
        ],
        scratch_shapes=scratch,
        compiler_params=_params(("parallel", "parallel")),
        name="s5_core",
    )(*args)


def _subln(o, g, out_scale):
    return o * lax.rsqrt(jnp.mean(o * o, axis=-1, keepdims=True) + LN_EPS) * g * out_scale


def _online_softmax_step(s, m, l, acc_ref, v):
    m_new = jnp.maximum(m, jnp.max(s, axis=-1, keepdims=True))
    corr = jnp.exp2(m - m_new)
    p = jnp.exp2(s - m_new)
    l_new = l * corr + jnp.sum(p, axis=-1, keepdims=True)
    acc_ref[...] = acc_ref[...] * corr + _dot(p.astype(BF16), v)
    return m_new, l_new


def _attn_prompt_body(slopes_ref, lam_ref, q_ref, k_ref, v_ref, g_ref, o_ref, bias_ref, s_ref, *, tq, nq, half,
                      out_scale):
    h = pl.program_id(1)
    neg_slope = -LOG2_E * slopes_ref[h]
    lam = lam_ref[0]
    rel = (lax.broadcasted_iota(jnp.int32, (tq, tq), 0) - lax.broadcasted_iota(jnp.int32, (tq, tq), 1)).astype(F32)
    bias_ref[0] = rel * neg_slope
    bias_ref[1] = jnp.where(rel >= 0, rel * neg_slope, NEG_INF)
    lane = lax.broadcasted_iota(jnp.int32, (tq, q_ref.shape[1]), 1)

    slot = 0
    for qi in range(nq):
        q = q_ref[qi * tq:(qi + 1) * tq, :]
        q1 = jnp.where(lane < half, q, jnp.zeros_like(q))
        q2 = jnp.where(lane >= half, q, jnp.zeros_like(q))
        offs = [neg_slope * float((qi - kj) * tq) for kj in range(qi + 1)]
        m1 = m2 = None
        for kj in range(qi + 1):
            kb = k_ref[kj * tq:(kj + 1) * tq, :]
            bias = bias_ref[1 if kj == qi else 0]
            s1 = _dot_nt(q1, kb) + bias
            s2 = _dot_nt(q2, kb) + bias
            s_ref[0, slot + kj] = s1
            s_ref[1, slot + kj] = s2
            x1 = jnp.max(s1, axis=-1, keepdims=True) + offs[kj]
            x2 = jnp.max(s2, axis=-1, keepdims=True) + offs[kj]
            m1 = x1 if m1 is None else jnp.maximum(m1, x1)
            m2 = x2 if m2 is None else jnp.maximum(m2, x2)
        acc1 = acc2 = l1 = l2 = None
        for kj in range(qi + 1):
            vb = v_ref[kj * tq:(kj + 1) * tq, :]
            p1 = jnp.exp2(s_ref[0, slot + kj] - (m1 - offs[kj]))
            p2 = jnp.exp2(s_ref[1, slot + kj] - (m2 - offs[kj]))
            d1 = _dot(p1.astype(BF16), vb)
            d2 = _dot(p2.astype(BF16), vb)
            r1 = jnp.sum(p1, axis=-1, keepdims=True)
            r2 = jnp.sum(p2, axis=-1, keepdims=True)
            acc1, acc2 = (d1, d2) if acc1 is None else (acc1 + d1, acc2 + d2)
            l1, l2 = (r1, r2) if l1 is None else (l1 + r1, l2 + r2)
        slot += qi + 1
        o = acc1 / l1 - lam * (acc2 / l2)
        o_ref[qi * tq:(qi + 1) * tq, :] = _subln(o, g_ref[...], out_scale).astype(o_ref.dtype)


def _attn_prompt(q, k, v, slopes, lam, subln_g3, aj, nb, seq, n_heads, out_scale):
    m, hd = q.shape
    dh = hd // n_heads
    tq = _pick(seq, (512, 256, 128, 64, 32, 16, 8))
    nq = seq // tq
    body = functools.partial(_attn_prompt_body, tq=tq, nq=nq, half=dh // 2, out_scale=out_scale)
    return pl.pallas_call(
        body,
        grid=(nb, n_heads),
        in_specs=[
            pl.BlockSpec(memory_space=pltpu.SMEM),
            pl.BlockSpec(memory_space=pltpu.SMEM),
            pl.BlockSpec((seq, dh), lambda b, h: (b, h)),
            pl.BlockSpec((seq, dh), lambda b, h: (b, h)),
            pl.BlockSpec((seq, dh), lambda b, h: (b, h)),
            pl.BlockSpec((None, 1, dh), lambda b, h: (aj, 0, 0)),
        ],
        out_specs=pl.BlockSpec((seq, dh), lambda b, h: (b, h)),
        out_shape=jax.ShapeDtypeStruct((m, hd), BF16),
        scratch_shapes=[pltpu.VMEM((2, tq, tq), F32), pltpu.VMEM((2, nq * (nq + 1) // 2, tq, tq), F32)],
        compiler_params=_params(("parallel", "parallel"), VMEM_LIMIT_SMALL),
        name="diff_attn_prompt",
    )(slopes, lam, q, k, v, subln_g3)


def _attn_decode_body(pt_ref, lam_ref, qc_ref, *refs, n_pages, page, gp, n_heads, t_new, out_scale):
    k_refs, v_refs = refs[:gp], refs[gp:2 * gp]
    (kn_ref, vn_ref, cmat_ref, cnew_ref, slope_ref, step_ref, g_ref, o_ref, acc_ref, m_ref, l_ref,
     s_ref) = refs[2 * gp:]
    p = pl.program_id(1)
    past = n_pages * page

    @pl.when(p == 0)
    def _():
        m_ref[...] = jnp.full_like(m_ref, NEG_INF)
        l_ref[...] = jnp.zeros_like(l_ref)
        acc_ref[...] = jnp.zeros_like(acc_ref)

    qc = qc_ref[...]
    slope = slope_ref[...]
    qpos = step_ref[...] + float(past)

    m_old = m_ref[...]
    m_new = m_old
    offs = []
    for g in range(gp):
        s = _dot_nt(qc, k_refs[g][...].astype(BF16)) + cmat_ref[...]
        s_ref[g] = s
        c = slope * (((p * gp + g) * page).astype(F32) - qpos)
        offs.append(c)
        m_new = jnp.maximum(m_new, jnp.max(s, axis=-1, keepdims=True) + c)
    lsum = jnp.zeros_like(m_new)
    pv = jnp.zeros(acc_ref.shape, F32)
    for g in range(gp):
        pe = jnp.exp2(s_ref[g] - (m_new - offs[g]))
        lsum = lsum + jnp.sum(pe, axis=-1, keepdims=True)
        pv = pv + _dot(pe.astype(BF16), v_refs[g][...].astype(BF16))
    corr = jnp.exp2(m_old - m_new)
    m_ref[...] = m_new
    l_ref[...] = l_ref[...] * corr + lsum
    acc_ref[...] = acc_ref[...] * corr + pv

    @pl.when(p == n_pages // gp - 1)
    def _():
        s = _dot_nt(qc, kn_ref[...]) + cnew_ref[...]
        m_fin, l_fin = _online_softmax_step(s, m_ref[...], l_ref[...], acc_ref, vn_ref[...])
        lam = lam_ref[0]
        inv = 1.0 / l_fin
        rph = acc_ref.shape[0] // n_heads
        dh = acc_ref.shape[1]
        for h in range(n_heads):
            blk = acc_ref[h * rph:(h + 1) * rph, :] * inv[h * rph:(h + 1) * rph, :]
            o = blk[0:t_new, :] - lam * blk[rph // 2:rph // 2 + t_new, :]
            o_ref[:, h * dh:(h + 1) * dh] = _subln(o, g_ref[...], out_scale)


def _attn_decode(q, k_new, v_new, cache_k4, cache_v4, page_table, slopes, lam, subln_g3, aj, nb, t_new, n_heads,
                 out_scale):
    hd = q.shape[1]
    dh = hd // n_heads
    half = dh // 2
    prows = cache_k4.shape[2]
    page = prows // n_heads
    n_pages = page_table.shape[1]
    gp = _pick(n_pages, (8, 4, 2, 1))
    rph = SUBLANES
    tslots = SUBLANES
    assert t_new <= rph // 2 and t_new <= tslots
    nr = n_heads * rph
    q4 = jnp.transpose(q.reshape(nb, t_new, n_heads, dh), (0, 2, 1, 3))
    q4 = jnp.pad(q4, ((0, 0), (0, 0), (0, rph // 2 - t_new), (0, 0)))
    lane = jnp.arange(dh)
    qc = jnp.stack([jnp.where(lane < half, q4, 0), jnp.where(lane >= half, q4, 0)], axis=2).reshape(nb, nr, dh)
    pad_tok = ((0, 0), (0, tslots - t_new), (0, 0), (0, 0))
    kn = jnp.pad(k_new.reshape(nb, t_new, n_heads, dh), pad_tok).reshape(nb, tslots * n_heads, dh)
    vn = jnp.pad(v_new.reshape(nb, t_new, n_heads, dh), pad_tok).reshape(nb, tslots * n_heads, dh)
    row = jnp.arange(nr)
    row_head, row_step = row // rph, (row % (rph // 2)).astype(F32)
    slope_rows = LOG2_E * slopes[row_head][:, None]
    step_rows = row_step[:, None]
    col = jnp.arange(prows)
    cmat = jnp.where((col % n_heads)[None, :] == row_head[:, None],
                     slope_rows * (col // n_heads).astype(F32)[None, :], NEG_INF)
    coln = jnp.arange(tslots * n_heads)
    tokn = (coln // n_heads).astype(F32)[None, :]
    ok = jnp.logical_and((coln % n_heads)[None, :] == row_head[:, None],
                         jnp.logical_and(tokn < t_new, tokn <= step_rows))
    cnew = jnp.where(ok, -slope_rows * (step_rows - tokn), NEG_INF)
    body = functools.partial(_attn_decode_body, n_pages=n_pages, page=page, gp=gp, n_heads=n_heads, t_new=t_new,
                             out_scale=out_scale)

    def page_spec(g):
        return pl.BlockSpec((None, None, prows, dh), lambda b, p, pt: (aj, pt[b * n_pages + p * gp + g], 0, 0))

    grid_spec = pltpu.PrefetchScalarGridSpec(
        num_scalar_prefetch=1,
        grid=(nb, n_pages // gp),
        in_specs=[
            pl.BlockSpec(memory_space=pltpu.SMEM),
            pl.BlockSpec((None, nr, dh), lambda b, p, pt: (b, 0, 0)),
            *[page_spec(g) for g in range(gp)],
            *[page_spec(g) for g in range(gp)],
            pl.BlockSpec((None, tslots * n_heads, dh), lambda b, p, pt: (b, 0, 0)),
            pl.BlockSpec((None, tslots * n_heads, dh), lambda b, p, pt: (b, 0, 0)),
            pl.BlockSpec((nr, prows), lambda b, p, pt: (0, 0)),
            pl.BlockSpec((nr, tslots * n_heads), lambda b, p, pt: (0, 0)),
            pl.BlockSpec((nr, 1), lambda b, p, pt: (0, 0)),
            pl.BlockSpec((nr, 1), lambda b, p, pt: (0, 0)),
            pl.BlockSpec((None, 1, dh), lambda b, p, pt: (aj, 0, 0)),
        ],
        out_specs=pl.BlockSpec((None, t_new, hd), lambda b, p, pt: (b, 0, 0)),
        scratch_shapes=[pltpu.VMEM((nr, dh), F32), pltpu.VMEM((nr, 1), F32), pltpu.VMEM((nr, 1), F32),
                        pltpu.VMEM((gp, nr, prows), F32)],
    )
    return pl.pallas_call(
        body,
        grid_spec=grid_spec,
        out_shape=jax.ShapeDtypeStruct((nb, t_new, hd), F32),
        compiler_params=_params(("parallel", "arbitrary")),
        name="diff_attn_decode",
    )(page_table.reshape(-1), lam, qc, *([cache_k4] * gp), *([cache_v4] * gp), kn, vn, cmat, cnew, slope_rows,
      step_rows, subln_g3)


def kernel(x_prompt, x_sample, cache_k, cache_v, state_s5_re, state_s5_im, page_table, p_prompt, p_sample,
           ln_g, ln_b, w_ffn_gate, w_ffn_up, w_ffn_down, w_pe, w_pe_gate,
           s5_lambda_re, s5_lambda_im, s5_log_dt, s5_b_re, s5_b_im, s5_c_re, s5_c_im, s5_d,
           s5_glu_w, s5_glu_b, s5_w_out,
           attn_w_qkv, attn_lambda_q1, attn_lambda_k1, attn_lambda_q2, attn_lambda_k2, attn_subln_g, attn_w_o):
    depth = ln_g.shape[0]
    alpha = (2.0 * depth) ** 0.25
    nbp, seq, d = x_prompt.shape
    nbs, t_new, _ = x_sample.shape
    n_attn, n_pool, page, n_heads, qk_dim = cache_k.shape
    v_dim = cache_v.shape[-1]
    half = qk_dim // 2
    attn_scale = 1.0 / math.sqrt(half)
    g_cnt, p_cnt = s5_lambda_re.shape[1:]
    gw = s5_b_re.shape[-1]
    gpc = S5_CHUNK // gw
    nc = d // S5_CHUNK
    hq = n_heads * qk_dim

    ln_g4 = ln_g.reshape(depth, ln_g.shape[1], 1, d)
    ln_b4 = ln_b.reshape(depth, ln_b.shape[1], 1, d)
    pp3 = p_prompt.reshape(depth, nbp * seq, -1)
    ps3 = p_sample.reshape(depth, nbs * t_new, -1)
    cache_k4 = cache_k.reshape(n_attn, n_pool, page * n_heads, qk_dim)
    cache_v4 = cache_v.reshape(n_attn, n_pool, page * n_heads, v_dim)
    attn_w_qkv = attn_w_qkv.astype(BF16)
    attn_w_o = attn_w_o.astype(BF16)
    s5_glu_w = s5_glu_w.astype(BF16)
    s5_w_out = s5_w_out.astype(BF16)
    w_pe_gate = w_pe_gate.astype(BF16)
    w_pe = w_pe.astype(BF16)
    glu_b3 = s5_glu_b.reshape(s5_glu_b.shape[0], 1, d)
    d_skip3 = s5_d.reshape(s5_d.shape[0], 1, d)
    subln_g3 = attn_subln_g.reshape(n_attn, 1, v_dim)
    slopes = jnp.exp2(-8.0 * jnp.arange(1, n_heads + 1, dtype=F32) / n_heads)

    hp = x_prompt.reshape(nbp * seq, d)
    hs = x_sample.reshape(nbs * t_new, d)
    kp = vp = None
    ksm, vsm = [], []
    srp, sip, srs, sis = [], [], [], []

    def split_state(st):
        lead = st.shape[:-2]
        st = st.reshape(lead + (nc, 2, gpc, p_cnt))
        return (st[..., 0, :, :].reshape(lead + (g_cnt, p_cnt)), st[..., 1, :, :].reshape(lead + (g_cnt, p_cnt)))

    for i in range(depth):
        j = i // 2
        hp, hs = _ffn_ln(hp, hs, w_ffn_gate, w_ffn_up, w_ffn_down, ln_g4, ln_b4, i, 0, 0, alpha)
        if i % 2 == 0:
            prm = (s5_lambda_re[j], s5_lambda_im[j], s5_log_dt[j], s5_b_re[j], s5_b_im[j], s5_c_re[j], s5_c_im[j])
            tabs_p = _s5_tables(*prm, long_len=seq)
            seg_len = seq // SUBLANES
            hperm = jnp.transpose(hp.reshape(nbp, SUBLANES, seg_len, d), (0, 2, 1, 3)).reshape(nbp, seq, d)
            yperm, stp = _s5_core(hperm, tabs_p, d_skip3, j)
            yp = jnp.transpose(yperm.reshape(nbp, seg_len, SUBLANES, d), (0, 2, 1, 3)).reshape(nbp * seq, d)
            re, im = split_state(stp[:, :, 0, :])
            srp.append(re)
            sip.append(im)
            hp = _glu_out_ln(yp, hp, s5_glu_w, glu_b3, s5_w_out, ln_g4, ln_b4, j, i, 1, alpha)
            tabs_s = _s5_tables(*prm, seg=t_new)
            s0 = jnp.concatenate([state_s5_re[j].reshape(nbs, nc, gpc * p_cnt),
                                  state_s5_im[j].reshape(nbs, nc, gpc * p_cnt)], axis=-1)
            s0_rows = jnp.zeros((nbs, t_new, nc, s0.shape[-1]), F32).at[:, 0].set(s0)
            s0_rows = jnp.transpose(s0_rows.reshape(nbs * t_new, nc, -1), (1, 0, 2))[None]
            ys, sts = _s5_core(hs.reshape(1, nbs * t_new, d), tabs_s, d_skip3, j, s0_rows)
            st_last = sts[0].reshape(nc, nbs, t_new, -1)[:, :, t_new - 1, :]
            re, im = split_state(jnp.transpose(st_last, (1, 0, 2)))
            srs.append(re)
            sis.append(im)
            hs = _glu_out_ln(ys.reshape(nbs * t_new, d), hs, s5_glu_w, glu_b3, s5_w_out, ln_g4, ln_b4, j, i, 1,
                             alpha)
        else:
            lam_init = 0.8 - 0.6 * math.exp(-0.3 * i)
            lam = (jnp.exp(jnp.sum(attn_lambda_q1[j] * attn_lambda_k1[j]))
                   - jnp.exp(jnp.sum(attn_lambda_q2[j] * attn_lambda_k2[j])) + lam_init).reshape(1).astype(F32)
            out_scale = 1.0 - lam_init
            qp, kp, kb, vp, vb = _qkv_proj(hp, attn_w_qkv, j, hq, hq, n_heads * v_dim, attn_scale * LOG2_E,
                                           n_slabs=n_attn, stacked=None if kp is None else (kp, vp))
            op = _attn_prompt(qp, kb, vb, slopes, lam, subln_g3, j, nbp, seq, n_heads, out_scale)
            hp = _proj_ln(op, hp, attn_w_o, j, ln_g4, ln_b4, i, 1, alpha)
            qs, kf, kb, vf, vb = _qkv_proj(hs, attn_w_qkv, j, hq, hq, n_heads * v_dim, attn_scale * LOG2_E)
            ksm.append(kf.reshape(nbs, t_new, n_heads, qk_dim))
            vsm.append(vf.reshape(nbs, t_new, n_heads, v_dim))
            osm = _attn_decode(qs, kb, vb, cache_k4, cache_v4, page_table, slopes, lam, subln_g3, j, nbs, t_new,
                               n_heads, out_scale)
            hs = _proj_ln(osm.reshape(nbs * t_new, n_heads * v_dim), hs, attn_w_o, j, ln_g4, ln_b4, i, 1, alpha)
        hp, hs = _ffn_ln(hp, hs, w_ffn_gate, w_ffn_up, w_ffn_down, ln_g4, ln_b4, i, 1, 2, alpha)
        hp = _pe_gate(hp, pp3, w_pe, w_pe_gate, i)
        hs = _pe_gate(hs, ps3, w_pe, w_pe_gate, i)

    return (hp.reshape(nbp, seq, d), hs.reshape(nbs, t_new, d),
            kp.reshape(n_attn, nbp, seq, n_heads, qk_dim), vp.reshape(n_attn, nbp, seq, n_heads, v_dim),
            jnp.stack(ksm), jnp.stack(vsm),
            jnp.stack(srp), jnp.stack(sip), jnp.stack(srs), jnp.stack(sis))
```
